```python
import math
import jax, jax.numpy as jnp
from jax import lax
import numpy as np

D_MODEL = 2048
BATCH = 4
SEQ = 2048
DEPTH = 1

N_HEADS_DIFF = 8
DIFF_HEAD_DIM = 64
DIFF_WIDTH = N_HEADS_DIFF * 2 * DIFF_HEAD_DIM
N_HEADS_MOBA = 8
MOBA_HEAD_DIM = 128
MOBA_WIDTH = N_HEADS_MOBA * MOBA_HEAD_DIM
MOBA_BLOCK = 256
MOBA_TOPK = 3
MOBA_Q_CHUNK = 32
ATTN_Q_BLOCK = 128
D_FF = 5632
RMS_EPS = 1e-6
IN_COLS = 3 * DIFF_WIDTH + 3 * MOBA_WIDTH + 2 * D_MODEL

kernel_name = "hybrid_diffattn_moba_gated_macaron"


def rmsnorm(x, g):
    xf = x.astype(jnp.float32)
    y = xf * lax.rsqrt(jnp.mean(xf * xf, axis=-1, keepdims=True) + RMS_EPS)
    return (y * g.astype(jnp.float32)).astype(x.dtype)


def swiglu(h, w_gu, w_down):
    gate, up = jnp.split(h @ w_gu, 2, axis=-1)
    return (jax.nn.silu(gate) * up) @ w_down


def alibi_slopes(n):
    return jnp.asarray(2.0 ** (-8.0 * np.arange(1, n + 1) / n), dtype=jnp.float32)


def diff_attention(q, k, v, lam, subln_g, lam_init):
    B, S, H, _, dh = q.shape
    q = q.transpose(0, 2, 3, 1, 4)
    k = k.transpose(0, 2, 3, 1, 4)
    v = v.transpose(0, 2, 1, 3)
    nq = S // ATTN_Q_BLOCK
    qb = q.reshape(B, H, 2, nq, ATTN_Q_BLOCK, dh).transpose(3, 0, 1, 2, 4, 5)
    slopes = alibi_slopes(H)
    kpos = jnp.arange(S)
    scale = dh ** -0.5

    def block(args):
        qblk, start = args
        s = jnp.einsum('bhmqd,bhmkd->bhmqk', qblk, k).astype(jnp.float32) * scale
        qpos = start + jnp.arange(ATTN_Q_BLOCK)
        dist = (qpos[:, None] - kpos[None, :]).astype(jnp.float32)
        s = s - (slopes[:, None, None] * dist)[None, :, None]
        s = jnp.where(dist >= 0, s, -jnp.inf)
        p = jax.nn.softmax(s, axis=-1)
        a = p[:, :, 0] - lam * p[:, :, 1]
        return jnp.einsum('bhqk,bhkd->bhqd', a.astype(v.dtype), v)

    starts = jnp.arange(nq, dtype=jnp.int32) * ATTN_Q_BLOCK
    o = lax.map(block, (qb, starts))
    o = o.transpose(1, 0, 3, 2, 4).reshape(B, S, H, 2 * dh)
    o = rmsnorm(o, subln_g) * (1.0 - lam_init)
    return o.reshape(B, S, H * 2 * dh)


def moba_attention(q, k, v):
    B, S, H, dh = q.shape
    nb = -(-S // MOBA_BLOCK)
    pad = nb * MOBA_BLOCK - S
    n_sel = min(MOBA_TOPK, nb - 1)
    scale = dh ** -0.5
    slopes = alibi_slopes(H)
    q = q.transpose(0, 2, 1, 3)
    k = jnp.pad(k.transpose(0, 2, 1, 3), ((0, 0), (0, 0), (0, pad), (0, 0)))
    v = jnp.pad(v.transpose(0, 2, 1, 3), ((0, 0), (0, 0), (0, pad), (0, 0)))
    kb = k.reshape(B, H, nb, MOBA_BLOCK, dh)
    vb = v.reshape(B, H, nb, MOBA_BLOCK, dh)
    kmean = jnp.mean(kb.astype(jnp.float32), axis=3)
    qblk = jnp.arange(S) // MOBA_BLOCK
    nc = S // MOBA_Q_CHUNK
    qc_all = q.reshape(B, H, nc, MOBA_Q_CHUNK, dh).transpose(2, 0, 1, 3, 4)
    starts = jnp.arange(nc, dtype=jnp.int32) * MOBA_Q_CHUNK

    if n_sel > 0:
        gscore = jnp.einsum('bhsd,bhnd->bhsn', q.astype(jnp.float32), kmean)
        past = jnp.arange(nb)[None, :] < qblk[:, None]
        gscore = jnp.where(past, gscore, -jnp.inf)
        _, idx = lax.top_k(gscore, n_sel)
        valid = idx < qblk[:, None]
        idx_all = idx.reshape(B, H, nc, MOBA_Q_CHUNK, n_sel).transpose(2, 0, 1, 3, 4)
        val_all = valid.reshape(B, H, nc, MOBA_Q_CHUNK, n_sel).transpose(2, 0, 1, 3, 4)
    else:
        idx_all = jnp.zeros((nc, B, H, MOBA_Q_CHUNK, 0), jnp.int32)
        val_all = jnp.zeros((nc, B, H, MOBA_Q_CHUNK, 0), bool)

    gather_blocks = jax.vmap(jax.vmap(lambda tbl, ix: tbl[ix]))

    def chunk(args):
        qc, idc, vdc, start = args
        qpos = start + jnp.arange(MOBA_Q_CHUNK)
        own = start // MOBA_BLOCK
        k_own = lax.dynamic_index_in_dim(kb, own, axis=2, keepdims=False)
        v_own = lax.dynamic_index_in_dim(vb, own, axis=2, keepdims=False)
        own_pos = own * MOBA_BLOCK + jnp.arange(MOBA_BLOCK)
        d_own = (qpos[:, None] - own_pos[None, :]).astype(jnp.float32)
        s_own = jnp.einsum('bhqd,bhkd->bhqk', qc, k_own).astype(jnp.float32) * scale
        s_own = jnp.where(d_own >= 0, s_own - slopes[None, :, None, None] * d_own, -jnp.inf)
        if n_sel == 0:
            p = jax.nn.softmax(s_own, axis=-1)
            return jnp.einsum('bhqk,bhkd->bhqd', p.astype(v_own.dtype), v_own)
        k_sel = gather_blocks(kb, idc)
        v_sel = gather_blocks(vb, idc)
        sel_pos = idc[..., None] * MOBA_BLOCK + jnp.arange(MOBA_BLOCK)
        d_sel = (qpos[None, None, :, None, None] - sel_pos).astype(jnp.float32)
        s_sel = jnp.einsum('bhqd,bhqjkd->bhqjk', qc, k_sel).astype(jnp.float32) * scale
        s_sel = s_sel - slopes[None, :, None, None, None] * d_sel
        s_sel = jnp.where(vdc[..., None], s_sel, -jnp.inf)
        Bq, Hq, C = s_own.shape[:3]
        s_all = jnp.concatenate([s_sel.reshape(Bq, Hq, C, n_sel * MOBA_BLOCK), s_own], axis=-1)
        p = jax.nn.softmax(s_all, axis=-1)
        p_sel = p[..., :n_sel * MOBA_BLOCK].reshape(Bq, Hq, C, n_sel, MOBA_BLOCK)
        p_own = p[..., n_sel * MOBA_BLOCK:]
        return (jnp.einsum('bhqjk,bhqjkd->bhqd', p_sel.astype(v_sel.dtype), v_sel)
                + jnp.einsum('bhqk,bhkd->bhqd', p_own.astype(v_own.dtype), v_own))

    o = lax.map(chunk, (qc_all, idx_all, val_all, starts))
    return o.transpose(1, 0, 3, 2, 4).reshape(B, S, H * dh)


def setup_inputs(seed: int = 0) -> dict:
    key = jax.random.key(seed)
    ks = jax.random.split(key, 20)
    f32 = jnp.float32
    L, D = DEPTH, D_MODEL

    def nrm(k, shape, fan_in):
        return jax.random.normal(k, shape, f32) * (fan_in ** -0.5)

    def gain(k, shape):
        return 1.0 + 0.02 * jax.random.normal(k, shape, f32)

    return {
        "x": jax.random.normal(ks[0], (BATCH, SEQ, D), f32),
        "g_ffn1": gain(ks[1], (L, D)),
        "w_ffn1_gu": nrm(ks[2], (L, D, 2 * D_FF), D),
        "w_ffn1_down": nrm(ks[3], (L, D_FF, D), D_FF),
        "g_mix": gain(ks[4], (L, D)),
        "w_in": nrm(ks[5], (L, D, IN_COLS), D),
        "lam_q1": 0.1 * jax.random.normal(ks[6], (L, DIFF_HEAD_DIM), f32),
        "lam_k1": 0.1 * jax.random.normal(ks[7], (L, DIFF_HEAD_DIM), f32),
        "lam_q2": 0.1 * jax.random.normal(ks[8], (L, DIFF_HEAD_DIM), f32),
        "lam_k2": 0.1 * jax.random.normal(ks[9], (L, DIFF_HEAD_DIM), f32),
        "g_subln": gain(ks[10], (L, 2 * DIFF_HEAD_DIM)),
        "p_a": nrm(ks[11], (L, DIFF_WIDTH, D), DIFF_WIDTH),
        "p_b": nrm(ks[12], (L, MOBA_WIDTH, D), MOBA_WIDTH),
        "w_o": nrm(ks[13], (L, D, D), D),
        "g_ffn2": gain(ks[14], (L, D)),
        "w_ffn2_gu": nrm(ks[15], (L, D, 2 * D_FF), D),
        "w_ffn2_down": nrm(ks[16], (L, D_FF, D), D_FF),
        "g_final": gain(ks[17], (D,)),
    }


def reference(x, g_ffn1, w_ffn1_gu, w_ffn1_down, g_mix, w_in, lam_q1, lam_k1, lam_q2, lam_k2,
              g_subln, p_a, p_b, w_o, g_ffn2, w_ffn2_gu, w_ffn2_down, g_final):
    B, S, D = x.shape
    c0 = 0
    offs = []
    for w in (DIFF_WIDTH, DIFF_WIDTH, DIFF_WIDTH, MOBA_WIDTH, MOBA_WIDTH, MOBA_WIDTH, D_MODEL):
        c0 += w
        offs.append(c0)
    for l in range(DEPTH):
        x = x + 0.5 * swiglu(rmsnorm(x, g_ffn1[l]), w_ffn1_gu[l], w_ffn1_down[l])
        h = rmsnorm(x, g_mix[l])
        proj = h @ w_in[l]
        qa, ka, va, qb, kb, vb, gate_a, gate_b = jnp.split(proj, offs, axis=-1)
        qa = qa.reshape(B, S, N_HEADS_DIFF, 2, DIFF_HEAD_DIM)
        ka = ka.reshape(B, S, N_HEADS_DIFF, 2, DIFF_HEAD_DIM)
        va = va.reshape(B, S, N_HEADS_DIFF, 2 * DIFF_HEAD_DIM)
        lam_init = 0.8 - 0.6 * math.exp(-0.3 * l)
        lam = (jnp.exp(jnp.sum(lam_q1[l].astype(jnp.float32) * lam_k1[l].astype(jnp.float32)))
               - jnp.exp(jnp.sum(lam_q2[l].astype(jnp.float32) * lam_k2[l].astype(jnp.float32)))
               + lam_init)
        o_a = diff_attention(qa, ka, va, lam, g_subln[l], lam_init)
        o_b = moba_attention(qb.reshape(B, S, N_HEADS_MOBA, MOBA_HEAD_DIM),
                             kb.reshape(B, S, N_HEADS_MOBA, MOBA_HEAD_DIM),
                             vb.reshape(B, S, N_HEADS_MOBA, MOBA_HEAD_DIM))
        merged = jax.nn.sigmoid(gate_a) * (o_a @ p_a[l]) + jax.nn.sigmoid(gate_b) * (o_b @ p_b[l])
        x = x + merged @ w_o[l]
        x = x + 0.5 * swiglu(rmsnorm(x, g_ffn2[l]), w_ffn2_gu[l], w_ffn2_down[l])
    return rmsnorm(x, g_final)
```

```python
import functools
import math

import jax
import jax.numpy as jnp
import numpy as np
from jax import lax
from jax.experimental import pallas as pl
from jax.experimental.pallas import tpu as pltpu

D_MODEL = 2048
N_HEADS_DIFF = 8
DIFF_HEAD_DIM = 64
DIFF_WIDTH = N_HEADS_DIFF * 2 * DIFF_HEAD_DIM
N_HEADS_MOBA = 8
MOBA_HEAD_DIM = 128
MOBA_WIDTH = N_HEADS_MOBA * MOBA_HEAD_DIM
MOBA_BLOCK = 256
MOBA_TOPK = 3
D_FF = 5632
RMS_EPS = 1e-6
QKV_COLS = 3 * DIFF_WIDTH + 3 * MOBA_WIDTH
LAM_INIT = 0.8 - 0.6 * math.exp(-0.3 * 0)

LANES = 128
VMEM_LIMIT_BYTES = 56 * 1024 * 1024
MASK_VALUE = -1e30

FFN_TOKEN_TILE = 512
FFN_FF_TILE = 512
PROJ_TOKEN_TILE = 1024
PROJ_COL_TILE = 1024
MIX_TOKEN_TILE = 512
MIX_COL_TILE = 512
ATTN_TILE = 256

_NT_DIMS = (((1,), (1,)), ((), ()))


def _rms_normalize(x, gain):
    ms = jnp.mean(x * x, axis=-1, keepdims=True)
    return x * lax.rsqrt(ms + RMS_EPS) * gain


def _compiler_params(semantics):
    return pltpu.CompilerParams(dimension_semantics=semantics, vmem_limit_bytes=VMEM_LIMIT_BYTES)


def _ffn_kernel(x_ref, g_ref, wg_ref, wu_ref, wd_ref, gf_ref, o_ref, h_ref, *, n_ff_tiles, final_norm):
    f = pl.program_id(1)

    @pl.when(f == 0)
    def _():
        x = x_ref[...]
        h_ref[...] = _rms_normalize(x, g_ref[...]).astype(jnp.bfloat16)
        o_ref[...] = x

    h = h_ref[...]
    gate = jnp.dot(h, wg_ref[...], preferred_element_type=jnp.float32)
    up = jnp.dot(h, wu_ref[...], preferred_element_type=jnp.float32)
    act = (gate * jax.nn.sigmoid(gate) * up * 0.5).astype(jnp.bfloat16)
    o_ref[...] += jnp.dot(act, wd_ref[...], preferred_element_type=jnp.float32)

    if final_norm:
        @pl.when(f == n_ff_tiles - 1)
        def _():
            o_ref[...] = _rms_normalize(o_ref[...], gf_ref[...])


def _ffn(x, gain, w_gu, w_down, final_gain, *, final_norm):
    t, d = x.shape
    d_ff = w_down.shape[0]
    tm, tf = FFN_TOKEN_TILE, FFN_FF_TILE
    n_ff_tiles = d_ff // tf
    assert t % tm == 0 and d_ff % tf == 0
    kern = functools.partial(_ffn_kernel, n_ff_tiles=n_ff_tiles, final_norm=final_norm)
    return pl.pallas_call(
        kern,
        grid=(t // tm, n_ff_tiles),
        in_specs=[
            pl.BlockSpec((tm, d), lambda i, f: (i, 0)),
            pl.BlockSpec((1, d), lambda i, f: (0, 0)),
            pl.BlockSpec((d, tf), lambda i, f: (0, f)),
            pl.BlockSpec((d, tf), lambda i, f: (0, f + n_ff_tiles)),
            pl.BlockSpec((tf, d), lambda i, f: (f, 0)),
            pl.BlockSpec((1, d), lambda i, f: (0, 0)),
        ],
        out_specs=pl.BlockSpec((tm, d), lambda i, f: (i, 0)),
        out_shape=jax.ShapeDtypeStruct((t, d), jnp.float32),
        scratch_shapes=[pltpu.VMEM((tm, d), jnp.bfloat16)],
        compiler_params=_compiler_params(("parallel", "arbitrary")),
        name="ffn_final" if final_norm else "ffn",
    )(x, gain, w_gu, w_gu, w_down, final_gain)


def _qkv_proj_kernel(x_ref, g_ref, w_ref, s_ref, o_ref, h_ref):
    @pl.when(pl.program_id(1) == 0)
    def _():
        h_ref[...] = _rms_normalize(x_ref[...], g_ref[...]).astype(jnp.bfloat16)

    acc = jnp.dot(h_ref[...], w_ref[...], preferred_element_type=jnp.float32)
    o_ref[...] = (acc * s_ref[...]).astype(o_ref.dtype)


def _qkv_proj(x, gain, w_in, col_scale):
    t, d = x.shape
    tm, tn = PROJ_TOKEN_TILE, PROJ_COL_TILE
    assert t % tm == 0 and QKV_COLS % tn == 0
    return pl.pallas_call(
        _qkv_proj_kernel,
        grid=(t // tm, QKV_COLS // tn),
        in_specs=[
            pl.BlockSpec((tm, d), lambda i, j: (i, 0)),
            pl.BlockSpec((1, d), lambda i, j: (0, 0)),
            pl.BlockSpec((d, tn), lambda i, j: (0, j)),
            pl.BlockSpec((1, tn), lambda i, j: (0, j)),
        ],
        out_specs=pl.BlockSpec((tm, tn), lambda i, j: (i, j)),
        out_shape=jax.ShapeDtypeStruct((t, QKV_COLS), jnp.bfloat16),
        scratch_shapes=[pltpu.VMEM((tm, d), jnp.bfloat16)],
        compiler_params=_compiler_params(("parallel", "arbitrary")),
        name="qkv_proj",
    )(x, gain, w_in, col_scale)


def _alibi_tiles(slope, tq, tk):
    r = lax.broadcasted_iota(jnp.int32, (tq, tk), 0)
    c = lax.broadcasted_iota(jnp.int32, (tq, tk), 1)
    bias = (c - r).astype(jnp.float32) * slope
    return bias, jnp.where(r >= c, bias, MASK_VALUE)


def _softmax_first(s, v):
    m = jnp.max(s, axis=-1, keepdims=True)
    p = jnp.exp(s - m)
    l = jnp.sum(p, axis=-1, keepdims=True)
    acc = jnp.dot(p.astype(v.dtype), v, preferred_element_type=jnp.float32)
    return m, l, acc


def _softmax_update(s, offset, v, m, l, acc):
    m_new = jnp.maximum(m, jnp.max(s, axis=-1, keepdims=True) + offset)
    p = jnp.exp(s - (m_new - offset))
    alpha = jnp.exp(m - m_new)
    l = alpha * l + jnp.sum(p, axis=-1, keepdims=True)
    acc = alpha * acc + jnp.dot(p.astype(v.dtype), v, preferred_element_type=jnp.float32)
    return m_new, l, acc


def _diff_attn_kernel(slopes_ref, lq1_ref, lk1_ref, lq2_ref, lk2_ref, gs_ref,
                      q_ref, k_ref, v_ref, o_ref):
    h = pl.program_id(1)
    i = pl.program_id(2)
    tq = tk = ATTN_TILE
    slope = slopes_ref[h]

    q = q_ref[0]
    lane = lax.broadcasted_iota(jnp.int32, q.shape, 1)
    zero = jnp.zeros_like(q)
    q_maps = (jnp.where(lane < DIFF_HEAD_DIM, q, zero), jnp.where(lane >= DIFF_HEAD_DIM, q, zero))
    bias, bias_diag = _alibi_tiles(slope, tq, tk)

    def kv_tile(n):
        start = pl.multiple_of(n * tk, tk)
        return k_ref[0, pl.ds(start, tk), :], v_ref[0, pl.ds(start, tk), :]

    kt, vt = kv_tile(i)
    state = []
    for qm in q_maps:
        s = lax.dot_general(qm, kt, _NT_DIMS, preferred_element_type=jnp.float32) + bias_diag
        state.extend(_softmax_first(s, vt))

    def past_tile(n, carry):
        kt, vt = kv_tile(n)
        offset = -slope * ((i - n) * tk).astype(jnp.float32)
        out = []
        for mi, qm in enumerate(q_maps):
            s = lax.dot_general(qm, kt, _NT_DIMS, preferred_element_type=jnp.float32) + bias
            out.extend(_softmax_update(s, offset, vt, *carry[3 * mi:3 * mi + 3]))
        return tuple(out)

    m0, l0, acc0, m1, l1, acc1 = lax.fori_loop(0, i, past_tile, tuple(state))

    lam = (jnp.exp(jnp.sum(lq1_ref[...] * lk1_ref[...], axis=-1, keepdims=True))
           - jnp.exp(jnp.sum(lq2_ref[...] * lk2_ref[...], axis=-1, keepdims=True))
           + LAM_INIT)
    o = acc0 / l0 - lam * (acc1 / l1)
    o_ref[0] = (_rms_normalize(o, gs_ref[...]) * (1.0 - LAM_INIT)).astype(o_ref.dtype)


def _diff_attn(qkv, slopes, lam_q1, lam_k1, lam_q2, lam_k2, g_subln):
    b, s, _ = qkv.shape
    nh, hw = N_HEADS_DIFF, 2 * DIFF_HEAD_DIM
    assert hw == LANES and s % ATTN_TILE == 0
    k_blk0, v_blk0 = DIFF_WIDTH // hw, 2 * DIFF_WIDTH // hw
    lam_spec = pl.BlockSpec((1, DIFF_HEAD_DIM), lambda bi, h, i: (0, 0))
    return pl.pallas_call(
        _diff_attn_kernel,
        grid=(b, nh, s // ATTN_TILE),
        in_specs=[
            pl.BlockSpec(memory_space=pltpu.SMEM),
            lam_spec, lam_spec, lam_spec, lam_spec,
            pl.BlockSpec((1, hw), lambda bi, h, i: (0, 0)),
            pl.BlockSpec((1, ATTN_TILE, hw), lambda bi, h, i: (bi, i, h)),
            pl.BlockSpec((1, s, hw), lambda bi, h, i: (bi, 0, k_blk0 + h)),
            pl.BlockSpec((1, s, hw), lambda bi, h, i: (bi, 0, v_blk0 + h)),
        ],
        out_specs=pl.BlockSpec((1, ATTN_TILE, hw), lambda bi, h, i: (bi, i, h)),
        out_shape=jax.ShapeDtypeStruct((b, s, DIFF_WIDTH), jnp.bfloat16),
        compiler_params=_compiler_params(("parallel", "parallel", "arbitrary")),
        name="diff_attn",
    )(slopes, lam_q1, lam_k1, lam_q2, lam_k2, g_subln, qkv, qkv, qkv)


def _moba_attn_kernel(slopes_ref, q_ref, k_ref, v_ref, o_ref, kaug_ref, kmean_hi_ref, kmean_lo_ref,
                      *, n_blocks, n_sel):
    h = pl.program_id(1)
    i = pl.program_id(2)
    blk, dh = MOBA_BLOCK, MOBA_HEAD_DIM
    slope = slopes_ref[h]

    @pl.when(i == 0)
    def _():
        k = k_ref[0]
        s_len = k.shape[0]
        kaug_ref[:, :dh] = k
        row_blk = lax.broadcasted_iota(jnp.int32, (s_len, LANES), 0) // blk
        lane = lax.broadcasted_iota(jnp.int32, (s_len, LANES), 1)
        kaug_ref[:, dh:] = (lane == row_blk).astype(kaug_ref.dtype)
        kmean = jnp.mean(k.astype(jnp.float32).reshape(n_blocks, blk, dh), axis=1)
        kmean = jnp.concatenate([kmean, jnp.zeros((LANES - n_blocks, dh), jnp.float32)], axis=0)
        hi = kmean.astype(jnp.bfloat16)
        kmean_hi_ref[...] = hi
        kmean_lo_ref[...] = (kmean - hi.astype(jnp.float32)).astype(jnp.bfloat16)

    q = q_ref[0]

    g = (lax.dot_general(q, kmean_hi_ref[...], _NT_DIMS, preferred_element_type=jnp.float32)
         + lax.dot_general(q, kmean_lo_ref[...], _NT_DIMS, preferred_element_type=jnp.float32))
    lane = lax.broadcasted_iota(jnp.int32, g.shape, 1)
    past = lane < i
    g = jnp.where(past, g, -jnp.inf)
    rank = jnp.zeros(g.shape, jnp.int32)
    for r in range(1, n_blocks):
        rank += (pltpu.roll(g, r, 1) >= g).astype(jnp.int32)
        rank += (pltpu.roll(g, LANES - r, 1) > g).astype(jnp.int32)
    keep = (past & (rank < n_sel)) | (lane == i)
    sel_bias = jnp.where(keep, 0.0, MASK_VALUE).astype(jnp.bfloat16)
    q_aug = jnp.concatenate([q, sel_bias], axis=1)

    bias, bias_diag = _alibi_tiles(slope, blk, blk)

    def kv_tile(n):
        start = pl.multiple_of(n * blk, blk)
        return kaug_ref[pl.ds(start, blk), :], v_ref[0, pl.ds(start, blk), :]

    kt, vt = kv_tile(i)
    s = lax.dot_general(q_aug, kt, _NT_DIMS, preferred_element_type=jnp.float32) + bias_diag
    state = _softmax_first(s, vt)

    def past_tile(n, carry):
        kt, vt = kv_tile(n)
        offset = -slope * ((i - n) * blk).astype(jnp.float32)
        s = lax.dot_general(q_aug, kt, _NT_DIMS, preferred_element_type=jnp.float32) + bias
        return _softmax_update(s, offset, vt, *carry)

    _, l, acc = lax.fori_loop(0, i, past_tile, state)
    o_ref[0] = (acc / l).astype(o_ref.dtype)


def _moba_attn(qkv, slopes):
    b, s, _ = qkv.shape
    nh, dh, blk = N_HEADS_MOBA, MOBA_HEAD_DIM, MOBA_BLOCK
    assert dh == LANES and s % blk == 0
    n_blocks = s // blk
    assert n_blocks <= LANES
    n_sel = min(MOBA_TOPK, n_blocks - 1)
    q_blk0 = 3 * DIFF_WIDTH // dh
    k_blk0, v_blk0 = q_blk0 + nh, q_blk0 + 2 * nh
    kern = functools.partial(_moba_attn_kernel, n_blocks=n_blocks, n_sel=n_sel)
    return pl.pallas_call(
        kern,
        grid=(b, nh, n_blocks),
        in_specs=[
            pl.BlockSpec(memory_space=pltpu.SMEM),
            pl.BlockSpec((1, blk, dh), lambda bi, h, i: (bi, i, q_blk0 + h)),
            pl.BlockSpec((1, s, dh), lambda bi, h, i: (bi, 0, k_blk0 + h)),
            pl.BlockSpec((1, s, dh), lambda bi, h, i: (bi, 0, v_blk0 + h)),
        ],
        out_specs=pl.BlockSpec((1, blk, dh), lambda bi, h, i: (bi, i, h)),
        out_shape=jax.ShapeDtypeStruct((b, s, MOBA_WIDTH), jnp.bfloat16),
        scratch_shapes=[
            pltpu.VMEM((s, dh + LANES), jnp.bfloat16),
            pltpu.VMEM((LANES, dh), jnp.bfloat16),
            pltpu.VMEM((LANES, dh), jnp.bfloat16),
        ],
        compiler_params=_compiler_params(("parallel", "parallel", "arbitrary")),
        name="moba_attn",
    )(slopes, qkv, qkv, qkv)


def _mix_out_kernel(x_ref, g_ref, oa_ref, ob_ref, wga_ref, wgb_ref, pa_ref, pb_ref, wo_ref, o_ref, h_ref):
    @pl.when(pl.program_id(1) == 0)
    def _():
        x = x_ref[...]
        h_ref[...] = _rms_normalize(x, g_ref[...]).astype(jnp.bfloat16)
        o_ref[...] = x

    h = h_ref[...]
    gate_a = jnp.dot(h, wga_ref[...], preferred_element_type=jnp.float32)
    gate_b = jnp.dot(h, wgb_ref[...], preferred_element_type=jnp.float32)
    proj_a = jnp.dot(oa_ref[...], pa_ref[...], preferred_element_type=jnp.float32)
    proj_b = jnp.dot(ob_ref[...], pb_ref[...], preferred_element_type=jnp.float32)
    merged = jax.nn.sigmoid(gate_a) * proj_a + jax.nn.sigmoid(gate_b) * proj_b
    o_ref[...] += jnp.dot(merged.astype(jnp.bfloat16), wo_ref[...], preferred_element_type=jnp.float32)


def _mix_out(x, gain, o_a, o_b, w_in, p_a, p_b, w_o):
    t, d = x.shape
    tm, tc = MIX_TOKEN_TILE, MIX_COL_TILE
    assert t % tm == 0 and d % tc == 0 and QKV_COLS % tc == 0
    ga_blk0 = QKV_COLS // tc
    gb_blk0 = ga_blk0 + d // tc
    return pl.pallas_call(
        _mix_out_kernel,
        grid=(t // tm, d // tc),
        in_specs=[
            pl.BlockSpec((tm, d), lambda i, c: (i, 0)),
            pl.BlockSpec((1, d), lambda i, c: (0, 0)),
            pl.BlockSpec((tm, DIFF_WIDTH), lambda i, c: (i, 0)),
            pl.BlockSpec((tm, MOBA_WIDTH), lambda i, c: (i, 0)),
            pl.BlockSpec((d, tc), lambda i, c: (0, ga_blk0 + c)),
            pl.BlockSpec((d, tc), lambda i, c: (0, gb_blk0 + c)),
            pl.BlockSpec((DIFF_WIDTH, tc), lambda i, c: (0, c)),
            pl.BlockSpec((MOBA_WIDTH, tc), lambda i, c: (0, c)),
            pl.BlockSpec((tc, d), lambda i, c: (c, 0)),
        ],
        out_specs=pl.BlockSpec((tm, d), lambda i, c: (i, 0)),
        out_shape=jax.ShapeDtypeStruct((t, d), jnp.float32),
        scratch_shapes=[pltpu.VMEM((tm, d), jnp.bfloat16)],
        compiler_params=_compiler_params(("parallel", "arbitrary")),
        name="mix_out",
    )(x, gain, o_a, o_b, w_in, w_in, p_a, p_b, w_o)


def _alibi_slopes(n):
    return jnp.asarray(2.0 ** (-8.0 * np.arange(1, n + 1) / n), dtype=jnp.float32)


def _qkv_col_scale():
    scale = np.ones((1, QKV_COLS), np.float32)
    scale[:, :DIFF_WIDTH] = DIFF_HEAD_DIM ** -0.5
    scale[:, 3 * DIFF_WIDTH:3 * DIFF_WIDTH + MOBA_WIDTH] = MOBA_HEAD_DIM ** -0.5
    return jnp.asarray(scale)


def kernel(x, g_ffn1, w_ffn1_gu, w_ffn1_down, g_mix, w_in, lam_q1, lam_k1, lam_q2, lam_k2, g_subln, p_a, p_b, w_o, g_ffn2, w_ffn2_gu, w_ffn2_down, g_final):
    b, s, d = x.shape
    assert g_ffn1.shape[0] == 1, "single-layer stack"
    bf16 = jnp.bfloat16
    xt = x.reshape(b * s, d)
    g_final_row = g_final.reshape(1, d)

    x1 = _ffn(xt, g_ffn1, w_ffn1_gu[0].astype(bf16), w_ffn1_down[0].astype(bf16), g_final_row,
              final_norm=False)

    w_in_bf = w_in[0].astype(bf16)
    qkv = _qkv_proj(x1, g_mix, w_in_bf, _qkv_col_scale()).reshape(b, s, QKV_COLS)
    o_a = _diff_attn(qkv, _alibi_slopes(N_HEADS_DIFF), lam_q1, lam_k1, lam_q2, lam_k2, g_subln)
    o_b = _moba_attn(qkv, _alibi_slopes(N_HEADS_MOBA))
    x2 = _mix_out(x1, g_mix, o_a.reshape(b * s, DIFF_WIDTH), o_b.reshape(b * s, MOBA_WIDTH),
                  w_in_bf, p_a[0].astype(bf16), p_b[0].astype(bf16), w_o[0].astype(bf16))

    out = _ffn(x2, g_ffn2, w_ffn2_gu[0].astype(bf16), w_ffn2_down[0].astype(bf16), g_final_row,
               final_norm=True)
    return out.reshape(b, s, d)
```

```python
import functools
import math

import jax
import jax.numpy as jnp
import numpy as np
from jax import lax
from jax.experimental import pallas as pl
from jax.experimental.pallas import tpu as pltpu

D_MODEL = 2048
N_HEADS_DIFF = 8
DIFF_HEAD_DIM = 64
DIFF_WIDTH = N_HEADS_DIFF * 2 * DIFF_HEAD_DIM
N_HEADS_MOBA = 8
MOBA_HEAD_DIM = 128
MOBA_WIDTH = N_HEADS_MOBA * MOBA_HEAD_DIM
MOBA_BLOCK = 256
MOBA_TOPK = 3
D_FF = 5632
RMS_EPS = 1e-6
QKV_COLS = 3 * DIFF_WIDTH + 3 * MOBA_WIDTH
LAM_INIT = 0.8 - 0.6 * math.exp(-0.3 * 0)

LANES = 128
SUBLANES = 8
VMEM_LIMIT_BYTES = 56 * 1024 * 1024
MASK_VALUE = -1e30

FFN_TOKEN_TILE = 512
FFN_FF_TILE = 512
PROJ_TOKEN_TILE = 1024
PROJ_COL_TILE = 1024
MIX_TOKEN_TILE = 512
MIX_COL_TILE = 512
ATTN_TILE = 256

_NT_DIMS = (((1,), (1,)), ((), ()))


def _rms_normalize(x, gain):
    ms = jnp.mean(x * x, axis=-1, keepdims=True)
    return x * lax.rsqrt(ms + RMS_EPS) * gain


def _compiler_params(semantics):
    return pltpu.CompilerParams(dimension_semantics=semantics, vmem_limit_bytes=VMEM_LIMIT_BYTES)


def _ffn_kernel(x_ref, g_ref, wg_ref, wu_ref, wd_ref, gf_ref, o_ref, h_ref, *, n_ff_tiles, final_norm):
    f = pl.program_id(1)

    @pl.when(f == 0)
    def _():
        x = x_ref[...]
        h_ref[...] = _rms_normalize(x, g_ref[...]).astype(jnp.bfloat16)
        o_ref[...] = x

    h = h_ref[...]
    gate = jnp.dot(h, wg_ref[...], preferred_element_type=jnp.float32)
    up = jnp.dot(h, wu_ref[...], preferred_element_type=jnp.float32)
    act = (gate * jax.nn.sigmoid(gate) * up * 0.5).astype(jnp.bfloat16)
    o_ref[...] += jnp.dot(act, wd_ref[...], preferred_element_type=jnp.float32)

    if final_norm:
        @pl.when(f == n_ff_tiles - 1)
        def _():
            o_ref[...] = _rms_normalize(o_ref[...], gf_ref[...])


def _ffn(x, gain, w_gu, w_down, final_gain, *, final_norm):
    t, d = x.shape
    d_ff = w_down.shape[0]
    tm, tf = FFN_TOKEN_TILE, FFN_FF_TILE
    n_ff_tiles = d_ff // tf
    assert t % tm == 0 and d_ff % tf == 0
    kern = functools.partial(_ffn_kernel, n_ff_tiles=n_ff_tiles, final_norm=final_norm)
    return pl.pallas_call(
        kern,
        grid=(t // tm, n_ff_tiles),
        in_specs=[
            pl.BlockSpec((tm, d), lambda i, f: (i, 0)),
            pl.BlockSpec((1, d), lambda i, f: (0, 0)),
            pl.BlockSpec((d, tf), lambda i, f: (0, f)),
            pl.BlockSpec((d, tf), lambda i, f: (0, f + n_ff_tiles)),
            pl.BlockSpec((tf, d), lambda i, f: (f, 0)),
            pl.BlockSpec((1, d), lambda i, f: (0, 0)),
        ],
        out_specs=pl.BlockSpec((tm, d), lambda i, f: (i, 0)),
        out_shape=jax.ShapeDtypeStruct((t, d), jnp.float32),
        scratch_shapes=[pltpu.VMEM((tm, d), jnp.bfloat16)],
        compiler_params=_compiler_params(("parallel", "arbitrary")),
        name="ffn_final" if final_norm else "ffn",
    )(x, gain, w_gu, w_gu, w_down, final_gain)


def _qkv_proj_kernel(x_ref, g_ref, w_ref, s_ref, o_ref, h_ref):
    @pl.when(pl.program_id(1) == 0)
    def _():
        h_ref[...] = _rms_normalize(x_ref[...], g_ref[...]).astype(jnp.bfloat16)

    acc = jnp.dot(h_ref[...], w_ref[...], preferred_element_type=jnp.float32)
    o_ref[...] = (acc * s_ref[...]).astype(o_ref.dtype)


def _qkv_proj(x, gain, w_in, col_scale):
    t, d = x.shape
    tm, tn = PROJ_TOKEN_TILE, PROJ_COL_TILE
    assert t % tm == 0 and QKV_COLS % tn == 0
    return pl.pallas_call(
        _qkv_proj_kernel,
        grid=(t // tm, QKV_COLS // tn),
        in_specs=[
            pl.BlockSpec((tm, d), lambda i, j: (i, 0)),
            pl.BlockSpec((1, d), lambda i, j: (0, 0)),
            pl.BlockSpec((d, tn), lambda i, j: (0, j)),
            pl.BlockSpec((1, tn), lambda i, j: (0, j)),
        ],
        out_specs=pl.BlockSpec((tm, tn), lambda i, j: (i, j)),
        out_shape=jax.ShapeDtypeStruct((t, QKV_COLS), jnp.bfloat16),
        scratch_shapes=[pltpu.VMEM((tm, d), jnp.bfloat16)],
        compiler_params=_compiler_params(("parallel", "arbitrary")),
        name="qkv_proj",
    )(x, gain, w_in, col_scale)


def _alibi_tiles_t(slope, t):
    kk = lax.broadcasted_iota(jnp.int32, (t, t), 0)
    qq = lax.broadcasted_iota(jnp.int32, (t, t), 1)
    bias = (kk - qq).astype(jnp.float32) * slope
    return bias, jnp.where(qq >= kk, bias, MASK_VALUE)


def _fold_rows(x, op):
    rows, cols = x.shape
    return op(x.reshape(rows // SUBLANES, SUBLANES, cols), axis=0)


def _score_tiles(k_tile, q, shifts, bias, bias_diag):
    n = len(shifts)
    tiles, cand = [], None
    for j in range(n):
        sj = lax.dot_general(k_tile(j), q, _NT_DIMS, preferred_element_type=jnp.float32)
        sj = sj + (bias_diag if j == n - 1 else bias)
        cj = _fold_rows(sj, jnp.max) + shifts[j]
        cand = cj if cand is None else jnp.maximum(cand, cj)
        tiles.append(sj)
    return tiles, jnp.max(cand, axis=0, keepdims=True)


def _softmax_pv(tiles, m, shifts, vt_tile):
    lpart, acc = None, None
    for j, sj in enumerate(tiles):
        p = jnp.exp2(sj - (m - shifts[j]))
        pj = _fold_rows(p, jnp.sum)
        lpart = pj if lpart is None else lpart + pj
        vt = vt_tile(j)
        pv = jnp.dot(vt, p.astype(vt.dtype), preferred_element_type=jnp.float32)
        acc = pv if acc is None else acc + pv
    return acc / jnp.sum(lpart, axis=0, keepdims=True)


def _run_pipelined(stage_a, stage_b, n):
    pending = stage_a(0)
    for u in range(n):
        nxt = stage_a(u + 1) if u + 1 < n else None
        stage_b(u, pending)
        pending = nxt


def _diff_attn_kernel(slopes_ref, lq1_ref, lk1_ref, lq2_ref, lk2_ref, gs_ref,
                      q_ref, k_ref, v_ref, o_ref, vt_ref):
    slope = slopes_ref[pl.program_id(1)]
    t = ATTN_TILE
    n_tiles = q_ref.shape[1] // t

    vt_ref[...] = v_ref[0].T
    bias, bias_diag = _alibi_tiles_t(slope, t)
    lam = (jnp.exp(jnp.sum(lq1_ref[...] * lk1_ref[...], axis=-1, keepdims=True))
           - jnp.exp(jnp.sum(lq2_ref[...] * lk2_ref[...], axis=-1, keepdims=True))
           + LAM_INIT)
    lane = lax.broadcasted_iota(jnp.int32, (t, LANES), 1)
    k_tile = lambda j: k_ref[0, j * t:(j + 1) * t, :]
    vt_tile = lambda j: vt_ref[:, j * t:(j + 1) * t]
    shifts = lambda c: [-slope * float((c - j) * t) for j in range(c + 1)]

    def scores(c):
        q = q_ref[0, c * t:(c + 1) * t, :]
        zero = jnp.zeros_like(q)
        return [_score_tiles(k_tile, qm, shifts(c), bias, bias_diag)
                for qm in (jnp.where(lane < DIFF_HEAD_DIM, q, zero), jnp.where(lane >= DIFF_HEAD_DIM, q, zero))]

    def finish(c, maps):
        o0, o1 = [_softmax_pv(tiles, m, shifts(c), vt_tile) for tiles, m in maps]
        o = (o0 - lam * o1).T
        o_ref[0, c * t:(c + 1) * t, :] = (
            _rms_normalize(o, gs_ref[...]) * (1.0 - LAM_INIT)).astype(o_ref.dtype)

    _run_pipelined(scores, finish, n_tiles)


def _diff_attn(qkv, slopes, lam_q1, lam_k1, lam_q2, lam_k2, g_subln):
    b, s, _ = qkv.shape
    nh, hw = N_HEADS_DIFF, 2 * DIFF_HEAD_DIM
    assert hw == LANES and s % ATTN_TILE == 0
    k_blk0, v_blk0 = DIFF_WIDTH // hw, 2 * DIFF_WIDTH // hw
    lam_spec = pl.BlockSpec((1, DIFF_HEAD_DIM), lambda bi, h: (0, 0))
    return pl.pallas_call(
        _diff_attn_kernel,
        grid=(b, nh),
        in_specs=[
            pl.BlockSpec(memory_space=pltpu.SMEM),
            lam_spec, lam_spec, lam_spec, lam_spec,
            pl.BlockSpec((1, hw), lambda bi, h: (0, 0)),
            pl.BlockSpec((1, s, hw), lambda bi, h: (bi, 0, h)),
            pl.BlockSpec((1, s, hw), lambda bi, h: (bi, 0, k_blk0 + h)),
            pl.BlockSpec((1, s, hw), lambda bi, h: (bi, 0, v_blk0 + h)),
        ],
        out_specs=pl.BlockSpec((1, s, hw), lambda bi, h: (bi, 0, h)),
        out_shape=jax.ShapeDtypeStruct((b, s, DIFF_WIDTH), jnp.bfloat16),
        scratch_shapes=[pltpu.VMEM((hw, s), jnp.bfloat16)],
        compiler_params=_compiler_params(("parallel", "parallel")),
        name="diff_attn",
    )(slopes, lam_q1, lam_k1, lam_q2, lam_k2, g_subln, qkv, qkv, qkv)


def _moba_attn_kernel(slopes_ref, q_ref, k_ref, v_ref, o_ref, vt_ref, *, n_blocks, n_sel):
    slope = slopes_ref[pl.program_id(1)]
    blk, dh = MOBA_BLOCK, MOBA_HEAD_DIM

    vt_ref[...] = v_ref[0].T
    bias, bias_diag = _alibi_tiles_t(slope, blk)

    kmean = jnp.mean(k_ref[0].astype(jnp.float32).reshape(n_blocks, blk, dh), axis=1)
    kmean = jnp.concatenate([kmean, jnp.zeros((LANES - n_blocks, dh), jnp.float32)], axis=0)
    kmean_hi = kmean.astype(jnp.bfloat16)
    kmean_lo = (kmean - kmean_hi.astype(jnp.float32)).astype(jnp.bfloat16)
    blk_id = lax.broadcasted_iota(jnp.int32, (n_blocks, blk), 0)
    k_tile = lambda j: k_ref[0, j * blk:(j + 1) * blk, :]
    vt_tile = lambda j: vt_ref[:, j * blk:(j + 1) * blk]

    def scores(c):
        q = q_ref[0, c * blk:(c + 1) * blk, :]
        g = (lax.dot_general(kmean_hi, q, _NT_DIMS, preferred_element_type=jnp.float32)
             + lax.dot_general(kmean_lo, q, _NT_DIMS, preferred_element_type=jnp.float32))[:n_blocks]
        past = blk_id < c
        g = jnp.where(past, g, -jnp.inf)
        rank = jnp.zeros(g.shape, jnp.int32)
        for r in range(1, n_blocks):
            other = pltpu.roll(g, r, 0)
            rank += jnp.where(blk_id >= r, (other >= g).astype(jnp.int32), (other > g).astype(jnp.int32))
        keep = (past & (rank < n_sel)) | (blk_id == c)
        sel_bias = jnp.where(keep, 0.0, MASK_VALUE)
        shifts = [sel_bias[j:j + 1, :] - slope * float((c - j) * blk) for j in range(c + 1)]
        return _score_tiles(k_tile, q, shifts, bias, bias_diag) + (shifts,)

    def finish(c, state):
        tiles, m, shifts = state
        o = _softmax_pv(tiles, m, shifts, vt_tile)
        o_ref[0, c * blk:(c + 1) * blk, :] = o.T.astype(o_ref.dtype)

    _run_pipelined(scores, finish, n_blocks)


def _moba_attn(qkv, slopes):
    b, s, _ = qkv.shape
    nh, dh, blk = N_HEADS_MOBA, MOBA_HEAD_DIM, MOBA_BLOCK
    assert dh == LANES and s % blk == 0
    n_blocks = s // blk
    assert n_blocks == SUBLANES, "block ranking uses one vreg row per MoBA block"
    n_sel = min(MOBA_TOPK, n_blocks - 1)
    q_blk0 = 3 * DIFF_WIDTH // dh
    k_blk0, v_blk0 = q_blk0 + nh, q_blk0 + 2 * nh
    kern = functools.partial(_moba_attn_kernel, n_blocks=n_blocks, n_sel=n_sel)
    return pl.pallas_call(
        kern,
        grid=(b, nh),
        in_specs=[
            pl.BlockSpec(memory_space=pltpu.SMEM),
            pl.BlockSpec((1, s, dh), lambda bi, h: (bi, 0, q_blk0 + h)),
            pl.BlockSpec((1, s, dh), lambda bi, h: (bi, 0, k_blk0 + h)),
            pl.BlockSpec((1, s, dh), lambda bi, h: (bi, 0, v_blk0 + h)),
        ],
        out_specs=pl.BlockSpec((1, s, dh), lambda bi, h: (bi, 0, h)),
        out_shape=jax.ShapeDtypeStruct((b, s, MOBA_WIDTH), jnp.bfloat16),
        scratch_shapes=[pltpu.VMEM((dh, s), jnp.bfloat16)],
        compiler_params=_compiler_params(("parallel", "parallel")),
        name="moba_attn",
    )(slopes, qkv, qkv, qkv)


def _mix_out_kernel(x_ref, g_ref, oa_ref, ob_ref, wga_ref, wgb_ref, pa_ref, pb_ref, wo_ref, o_ref, h_ref):
    @pl.when(pl.program_id(1) == 0)
    def _():
        x = x_ref[...]
        h_ref[...] = _rms_normalize(x, g_ref[...]).astype(jnp.bfloat16)
        o_ref[...] = x

    h = h_ref[...]
    gate_a = jnp.dot(h, wga_ref[...], preferred_element_type=jnp.float32)
    gate_b = jnp.dot(h, wgb_ref[...], preferred_element_type=jnp.float32)
    proj_a = jnp.dot(oa_ref[...], pa_ref[...], preferred_element_type=jnp.float32)
    proj_b = jnp.dot(ob_ref[...], pb_ref[...], preferred_element_type=jnp.float32)
    merged = jax.nn.sigmoid(gate_a) * proj_a + jax.nn.sigmoid(gate_b) * proj_b
    o_ref[...] += jnp.dot(merged.astype(jnp.bfloat16), wo_ref[...], preferred_element_type=jnp.float32)


def _mix_out(x, gain, o_a, o_b, w_in, p_a, p_b, w_o):
    t, d = x.shape
    tm, tc = MIX_TOKEN_TILE, MIX_COL_TILE
    assert t % tm == 0 and d % tc == 0 and QKV_COLS % tc == 0
    ga_blk0 = QKV_COLS // tc
    gb_blk0 = ga_blk0 + d // tc
    return pl.pallas_call(
        _mix_out_kernel,
        grid=(t // tm, d // tc),
        in_specs=[
            pl.BlockSpec((tm, d), lambda i, c: (i, 0)),
            pl.BlockSpec((1, d), lambda i, c: (0, 0)),
            pl.BlockSpec((tm, DIFF_WIDTH), lambda i, c: (i, 0)),
            pl.BlockSpec((tm, MOBA_WIDTH), lambda i, c: (i, 0)),
            pl.BlockSpec((d, tc), lambda i, c: (0, ga_blk0 + c)),
            pl.BlockSpec((d, tc), lambda i, c: (0, gb_blk0 + c)),
            pl.BlockSpec((DIFF_WIDTH, tc), lambda i, c: (0, c)),
            pl.BlockSpec((MOBA_WIDTH, tc), lambda i, c: (0, c)),
            pl.BlockSpec((tc, d), lambda i, c: (c, 0)),
        ],
        out_specs=pl.BlockSpec((tm, d), lambda i, c: (i, 0)),
        out_shape=jax.ShapeDtypeStruct((t, d), jnp.float32),
        scratch_shapes=[pltpu.VMEM((tm, d), jnp.bfloat16)],
        compiler_params=_compiler_params(("parallel", "arbitrary")),
        name="mix_out",
    )(x, gain, o_a, o_b, w_in, w_in, p_a, p_b, w_o)


LOG2_E = math.log2(math.e)


def _alibi_slopes(n):
    return jnp.asarray(LOG2_E * 2.0 ** (-8.0 * np.arange(1, n + 1) / n), dtype=jnp.float32)


def _qkv_col_scale():
    scale = np.ones((1, QKV_COLS), np.float32)
    scale[:, :DIFF_WIDTH] = LOG2_E * DIFF_HEAD_DIM ** -0.5
    scale[:, 3 * DIFF_WIDTH:3 * DIFF_WIDTH + MOBA_WIDTH] = LOG2_E * MOBA_HEAD_DIM ** -0.5
    return jnp.asarray(scale)


def kernel(x, g_ffn1, w_ffn1_gu, w_ffn1_down, g_mix, w_in, lam_q1, lam_k1, lam_q2, lam_k2, g_subln, p_a, p_b, w_o, g_ffn2, w_ffn2_gu, w_ffn2_down, g_final):
    b, s, d = x.shape
    assert g_ffn1.shape[0] == 1, "single-layer stack"
    bf16 = jnp.bfloat16
    xt = x.reshape(b * s, d)
    g_final_row = g_final.reshape(1, d)

    x1 = _ffn(xt, g_ffn1, w_ffn1_gu[0].astype(bf16), w_ffn1_down[0].astype(bf16), g_final_row,
              final_norm=False)

    w_in_bf = w_in[0].astype(bf16)
    qkv = _qkv_proj(x1, g_mix, w_in_bf, _qkv_col_scale()).reshape(b, s, QKV_COLS)
    o_a = _diff_attn(qkv, _alibi_slopes(N_HEADS_DIFF), lam_q1, lam_k1, lam_q2, lam_k2, g_subln)
    o_b = _moba_attn(qkv, _alibi_slopes(N_HEADS_MOBA))
    x2 = _mix_out(x1, g_mix, o_a.reshape(b * s, DIFF_WIDTH), o_b.reshape(b * s, MOBA_WIDTH),
                  w_in_bf, p_a[0].astype(bf16), p_b[0].astype(bf16), w_o[0].astype(bf16))

    out = _ffn(x2, g_ffn2, w_ffn2_gu[0].astype(bf16), w_ffn2_down[0].astype(bf16), g_final_row,
               final_norm=True)
    return out.reshape(b, s, d)
```

```python
import functools
import math

import jax
import jax.numpy as jnp
import numpy as np
from jax import lax
from jax.experimental import pallas as pl
from jax.experimental.pallas import tpu as pltpu

D_MODEL = 2048
N_HEADS_DIFF = 8
DIFF_HEAD_DIM = 64
DIFF_WIDTH = N_HEADS_DIFF * 2 * DIFF_HEAD_DIM
N_HEADS_MOBA = 8
MOBA_HEAD_DIM = 128
MOBA_WIDTH = N_HEADS_MOBA * MOBA_HEAD_DIM
MOBA_BLOCK = 256
MOBA_TOPK = 3
D_FF = 5632
RMS_EPS = 1e-6
QKV_COLS = 3 * DIFF_WIDTH + 3 * MOBA_WIDTH
LAM_INIT = 0.8 - 0.6 * math.exp(-0.3 * 0)

LANES = 128
SUBLANES = 8
VMEM_LIMIT_BYTES = 56 * 1024 * 1024
MASK_VALUE = -1e30

FFN_TOKEN_TILE = 1024
FFN_FF_TILE = 256
PROJ_TOKEN_TILE = 1024
PROJ_COL_TILE = 1024
MIX_TOKEN_TILE = 512
MIX_COL_TILE = 512
ATTN_TILE = 256

_NT_DIMS = (((1,), (1,)), ((), ()))


def _rms_normalize(x, gain):
    ms = jnp.mean(x * x, axis=-1, keepdims=True)
    return x * lax.rsqrt(ms + RMS_EPS) * gain


def _compiler_params(semantics):
    return pltpu.CompilerParams(dimension_semantics=semantics, vmem_limit_bytes=VMEM_LIMIT_BYTES)


def _ffn_kernel(x_ref, g_ref, wg_ref, wu_ref, wd_ref, gf_ref, o_ref, h_ref, *, n_ff_tiles, final_norm):
    f = pl.program_id(1)

    @pl.when(f == 0)
    def _():
        x = x_ref[...]
        h_ref[...] = _rms_normalize(x, g_ref[...]).astype(jnp.bfloat16)
        o_ref[...] = x

    h = h_ref[...]
    bf16 = jnp.bfloat16
    gate = jnp.dot(h, wg_ref[...].astype(bf16), preferred_element_type=jnp.float32)
    up = jnp.dot(h, wu_ref[...].astype(bf16), preferred_element_type=jnp.float32)
    act = (gate * jax.nn.sigmoid(gate) * up * 0.5).astype(bf16)
    o_ref[...] += jnp.dot(act, wd_ref[...].astype(bf16), preferred_element_type=jnp.float32)

    if final_norm:
        @pl.when(f == n_ff_tiles - 1)
        def _():
            o_ref[...] = _rms_normalize(o_ref[...], gf_ref[...])


def _ffn(x, gain, w_gu, w_down, final_gain, *, final_norm):
    t, d = x.shape
    d_ff = w_down.shape[0]
    tm, tf = FFN_TOKEN_TILE, FFN_FF_TILE
    n_ff_tiles = d_ff // tf
    assert t % tm == 0 and d_ff % tf == 0
    kern = functools.partial(_ffn_kernel, n_ff_tiles=n_ff_tiles, final_norm=final_norm)
    return pl.pallas_call(
        kern,
        grid=(t // tm, n_ff_tiles),
        in_specs=[
            pl.BlockSpec((tm, d), lambda i, f: (i, 0)),
            pl.BlockSpec((1, d), lambda i, f: (0, 0)),
            pl.BlockSpec((d, tf), lambda i, f: (0, f)),
            pl.BlockSpec((d, tf), lambda i, f: (0, f + n_ff_tiles)),
            pl.BlockSpec((tf, d), lambda i, f: (f, 0)),
            pl.BlockSpec((1, d), lambda i, f: (0, 0)),
        ],
        out_specs=pl.BlockSpec((tm, d), lambda i, f: (i, 0)),
        out_shape=jax.ShapeDtypeStruct((t, d), jnp.float32),
        scratch_shapes=[pltpu.VMEM((tm, d), jnp.bfloat16)],
        compiler_params=_compiler_params(("parallel", "arbitrary")),
        name="ffn_final" if final_norm else "ffn",
    )(x, gain, w_gu, w_gu, w_down, final_gain)


def _qkv_proj_kernel(x_ref, g_ref, w_ref, s_ref, o_ref, h_ref):
    @pl.when(pl.program_id(1) == 0)
    def _():
        h_ref[...] = _rms_normalize(x_ref[...], g_ref[...]).astype(jnp.bfloat16)

    acc = jnp.dot(h_ref[...], w_ref[...].astype(jnp.bfloat16), preferred_element_type=jnp.float32)
    o_ref[...] = (acc * s_ref[...]).astype(o_ref.dtype)


def _qkv_proj(x, gain, w_in, col_scale):
    t, d = x.shape
    tm, tn = PROJ_TOKEN_TILE, PROJ_COL_TILE
    assert t % tm == 0 and QKV_COLS % tn == 0
    return pl.pallas_call(
        _qkv_proj_kernel,
        grid=(t // tm, QKV_COLS // tn),
        in_specs=[
            pl.BlockSpec((tm, d), lambda i, j: (i, 0)),
            pl.BlockSpec((1, d), lambda i, j: (0, 0)),
            pl.BlockSpec((d, tn), lambda i, j: (0, j)),
            pl.BlockSpec((1, tn), lambda i, j: (0, j)),
        ],
        out_specs=pl.BlockSpec((tm, tn), lambda i, j: (i, j)),
        out_shape=jax.ShapeDtypeStruct((t, QKV_COLS), jnp.bfloat16),
        scratch_shapes=[pltpu.VMEM((tm, d), jnp.bfloat16)],
        compiler_params=_compiler_params(("parallel", "arbitrary")),
        name="qkv_proj",
    )(x, gain, w_in, col_scale)


ALIBI_TERMS = 3


def _alibi_columns(slope, t):
    lane = lax.broadcasted_iota(jnp.int32, (t, LANES), 1)
    rest = lax.broadcasted_iota(jnp.int32, (t, LANES), 0).astype(jnp.float32) * slope
    k_extra = jnp.zeros((t, LANES), jnp.float32)
    for i in range(ALIBI_TERMS):
        term = rest.astype(jnp.bfloat16).astype(jnp.float32)
        k_extra = jnp.where(lane == i, term, k_extra)
        rest = rest - term
    q_extra = jnp.where(lane < ALIBI_TERMS, 1.0, 0.0)
    return k_extra.astype(jnp.bfloat16), q_extra.astype(jnp.bfloat16)


def _query_shift(slope, t):
    return lax.broadcasted_iota(jnp.int32, (1, t), 1).astype(jnp.float32) * (-slope)


def _causal_mask_t(t):
    kk = lax.broadcasted_iota(jnp.int32, (t, t), 0)
    qq = lax.broadcasted_iota(jnp.int32, (t, t), 1)
    return jnp.where(qq >= kk, 0.0, MASK_VALUE)


def _fold_rows(x, op):
    rows, cols = x.shape
    return op(x.reshape(rows // SUBLANES, SUBLANES, cols), axis=0)


def _score_tiles(k_tile, q, shifts, mask_diag, s_ref):
    n, t = len(shifts), q.shape[0]
    cand = None
    for j in range(n):
        sj = lax.dot_general(k_tile(j), q, _NT_DIMS, preferred_element_type=jnp.float32)
        if j == n - 1:
            sj = sj + mask_diag
        s_ref[j * t:(j + 1) * t, :] = sj
        cj = _fold_rows(sj, jnp.max) + shifts[j]
        cand = cj if cand is None else jnp.maximum(cand, cj)
    return jnp.max(cand, axis=0, keepdims=True)


def _softmax_pv(s_ref, m, shifts, vt_tile):
    t = m.shape[1]
    lpart, acc = None, None
    for j in range(len(shifts)):
        sj = s_ref[j * t:(j + 1) * t, :]
        p = jnp.exp2(sj - (m - shifts[j]))
        pj = _fold_rows(p, jnp.sum)
        lpart = pj if lpart is None else lpart + pj
        vt = vt_tile(j)
        pv = jnp.dot(vt, p.astype(vt.dtype), preferred_element_type=jnp.float32)
        acc = pv if acc is None else acc + pv
    return acc / jnp.sum(lpart, axis=0, keepdims=True)


def _run_pipelined(stage_a, stage_b, n):
    pending = stage_a(0)
    for u in range(n):
        nxt = stage_a(u + 1) if u + 1 < n else None
        stage_b(u, pending)
        pending = nxt


def _diff_attn_kernel(slopes_ref, lq1_ref, lk1_ref, lq2_ref, lk2_ref, gs_ref,
                      q_ref, k_ref, v_ref, o_ref, vt_ref, s_ref):
    slope = slopes_ref[pl.program_id(1)]
    t = ATTN_TILE
    n_tiles = q_ref.shape[1] // t

    vt_ref[...] = v_ref[0].T
    k_extra, q_extra = _alibi_columns(slope, t)
    mask_diag = _causal_mask_t(t)
    q_shift = _query_shift(slope, t)
    lam = (jnp.exp(jnp.sum(lq1_ref[...] * lk1_ref[...], axis=-1, keepdims=True))
           - jnp.exp(jnp.sum(lq2_ref[...] * lk2_ref[...], axis=-1, keepdims=True))
           + LAM_INIT)
    lane = lax.broadcasted_iota(jnp.int32, (t, LANES), 1)
    k_tile = lambda j: jnp.concatenate([k_ref[0, j * t:(j + 1) * t, :], k_extra], axis=1)
    vt_tile = lambda j: vt_ref[:, j * t:(j + 1) * t]
    shifts = lambda c: [q_shift - slope * float((c - j) * t) for j in range(c + 1)]

    def scores(c):
        q = q_ref[0, c * t:(c + 1) * t, :]
        zero = jnp.zeros_like(q)
        q_maps = (jnp.where(lane < DIFF_HEAD_DIM, q, zero), jnp.where(lane >= DIFF_HEAD_DIM, q, zero))
        return [_score_tiles(k_tile, jnp.concatenate([qm, q_extra], axis=1), shifts(c), mask_diag,
                             s_ref.at[c % 2, mi])
                for mi, qm in enumerate(q_maps)]

    def finish(c, maxes):
        o0, o1 = [_softmax_pv(s_ref.at[c % 2, mi], m, shifts(c), vt_tile) for mi, m in enumerate(maxes)]
        o = (o0 - lam * o1).T
        o_ref[0, c * t:(c + 1) * t, :] = (
            _rms_normalize(o, gs_ref[...]) * (1.0 - LAM_INIT)).astype(o_ref.dtype)

    _run_pipelined(scores, finish, n_tiles)


def _diff_attn(qkv, slopes, lam_q1, lam_k1, lam_q2, lam_k2, g_subln):
    b, s, _ = qkv.shape
    nh, hw = N_HEADS_DIFF, 2 * DIFF_HEAD_DIM
    assert hw == LANES and s % ATTN_TILE == 0
    k_blk0, v_blk0 = DIFF_WIDTH // hw, 2 * DIFF_WIDTH // hw
    lam_spec = pl.BlockSpec((1, DIFF_HEAD_DIM), lambda bi, h: (0, 0))
    return pl.pallas_call(
        _diff_attn_kernel,
        grid=(b, nh),
        in_specs=[
            pl.BlockSpec(memory_space=pltpu.SMEM),
            lam_spec, lam_spec, lam_spec, lam_spec,
            pl.BlockSpec((1, hw), lambda bi, h: (0, 0)),
            pl.BlockSpec((1, s, hw), lambda bi, h: (bi, 0, h)),
            pl.BlockSpec((1, s, hw), lambda bi, h: (bi, 0, k_blk0 + h)),
            pl.BlockSpec((1, s, hw), lambda bi, h: (bi, 0, v_blk0 + h)),
        ],
        out_specs=pl.BlockSpec((1, s, hw), lambda bi, h: (bi, 0, h)),
        out_shape=jax.ShapeDtypeStruct((b, s, DIFF_WIDTH), jnp.bfloat16),
        scratch_shapes=[pltpu.VMEM((hw, s), jnp.bfloat16),
                        pltpu.VMEM((2, 2, s, ATTN_TILE), jnp.float32)],
        compiler_params=_compiler_params(("parallel", "parallel")),
        name="diff_attn",
    )(slopes, lam_q1, lam_k1, lam_q2, lam_k2, g_subln, qkv, qkv, qkv)


def _moba_attn_kernel(slopes_ref, q_ref, k_ref, v_ref, o_ref, vt_ref, s_ref, *, n_blocks, n_sel):
    slope = slopes_ref[pl.program_id(1)]
    blk, dh = MOBA_BLOCK, MOBA_HEAD_DIM

    vt_ref[...] = v_ref[0].T
    k_extra, q_extra = _alibi_columns(slope, blk)
    mask_diag = _causal_mask_t(blk)
    q_shift = _query_shift(slope, blk)

    kmean = jnp.mean(k_ref[0].astype(jnp.float32).reshape(n_blocks, blk, dh), axis=1)
    kmean = jnp.concatenate([kmean, jnp.zeros((LANES - n_blocks, dh), jnp.float32)], axis=0)
    kmean_hi = kmean.astype(jnp.bfloat16)
    kmean_lo = (kmean - kmean_hi.astype(jnp.float32)).astype(jnp.bfloat16)
    blk_id = lax.broadcasted_iota(jnp.int32, (n_blocks, blk), 0)
    k_tile = lambda j: jnp.concatenate([k_ref[0, j * blk:(j + 1) * blk, :], k_extra], axis=1)
    vt_tile = lambda j: vt_ref[:, j * blk:(j + 1) * blk]

    def scores(c):
        q = q_ref[0, c * blk:(c + 1) * blk, :]
        g = (lax.dot_general(kmean_hi, q, _NT_DIMS, preferred_element_type=jnp.float32)
             + lax.dot_general(kmean_lo, q, _NT_DIMS, preferred_element_type=jnp.float32))[:n_blocks]
        past = blk_id < c
        g = jnp.where(past, g, -jnp.inf)
        rank = jnp.zeros(g.shape, jnp.int32)
        for r in range(1, n_blocks):
            other = pltpu.roll(g, r, 0)
            rank += jnp.where(blk_id >= r, (other >= g).astype(jnp.int32), (other > g).astype(jnp.int32))
        keep = (past & (rank < n_sel)) | (blk_id == c)
        sel_bias = jnp.where(keep, 0.0, MASK_VALUE)
        shifts = [sel_bias[j:j + 1, :] + (q_shift - slope * float((c - j) * blk)) for j in range(c + 1)]
        q_aug = jnp.concatenate([q, q_extra], axis=1)
        return _score_tiles(k_tile, q_aug, shifts, mask_diag, s_ref.at[c % 2]), shifts

    def finish(c, state):
        m, shifts = state
        o = _softmax_pv(s_ref.at[c % 2], m, shifts, vt_tile)
        o_ref[0, c * blk:(c + 1) * blk, :] = o.T.astype(o_ref.dtype)

    _run_pipelined(scores, finish, n_blocks)


def _moba_attn(qkv, slopes):
    b, s, _ = qkv.shape
    nh, dh, blk = N_HEADS_MOBA, MOBA_HEAD_DIM, MOBA_BLOCK
    assert dh == LANES and s % blk == 0
    n_blocks = s // blk
    assert n_blocks == SUBLANES, "block ranking uses one vreg row per MoBA block"
    n_sel = min(MOBA_TOPK, n_blocks - 1)
    q_blk0 = 3 * DIFF_WIDTH // dh
    k_blk0, v_blk0 = q_blk0 + nh, q_blk0 + 2 * nh
    kern = functools.partial(_moba_attn_kernel, n_blocks=n_blocks, n_sel=n_sel)
    return pl.pallas_call(
        kern,
        grid=(b, nh),
        in_specs=[
            pl.BlockSpec(memory_space=pltpu.SMEM),
            pl.BlockSpec((1, s, dh), lambda bi, h: (bi, 0, q_blk0 + h)),
            pl.BlockSpec((1, s, dh), lambda bi, h: (bi, 0, k_blk0 + h)),
            pl.BlockSpec((1, s, dh), lambda bi, h: (bi, 0, v_blk0 + h)),
        ],
        out_specs=pl.BlockSpec((1, s, dh), lambda bi, h: (bi, 0, h)),
        out_shape=jax.ShapeDtypeStruct((b, s, MOBA_WIDTH), jnp.bfloat16),
        scratch_shapes=[pltpu.VMEM((dh, s), jnp.bfloat16),
                        pltpu.VMEM((2, s, blk), jnp.float32)],
        compiler_params=_compiler_params(("parallel", "parallel")),
        name="moba_attn",
    )(slopes, qkv, qkv, qkv)


def _mix_out_kernel(x_ref, g_ref, oa_ref, ob_ref, wga_ref, wgb_ref, pa_ref, pb_ref, wo_ref, o_ref, h_ref):
    @pl.when(pl.program_id(1) == 0)
    def _():
        x = x_ref[...]
        h_ref[...] = _rms_normalize(x, g_ref[...]).astype(jnp.bfloat16)
        o_ref[...] = x

    h = h_ref[...]
    gate_a = jnp.dot(h, wga_ref[...], preferred_element_type=jnp.float32)
    gate_b = jnp.dot(h, wgb_ref[...], preferred_element_type=jnp.float32)
    proj_a = jnp.dot(oa_ref[...], pa_ref[...], preferred_element_type=jnp.float32)
    proj_b = jnp.dot(ob_ref[...], pb_ref[...], preferred_element_type=jnp.float32)
    merged = jax.nn.sigmoid(gate_a) * proj_a + jax.nn.sigmoid(gate_b) * proj_b
    o_ref[...] += jnp.dot(merged.astype(jnp.bfloat16), wo_ref[...], preferred_element_type=jnp.float32)


def _mix_out(x, gain, o_a, o_b, w_gates, p_a, p_b, w_o):
    t, d = x.shape
    tm, tc = MIX_TOKEN_TILE, MIX_COL_TILE
    assert t % tm == 0 and d % tc == 0
    ga_blk0, gb_blk0 = 0, d // tc
    return pl.pallas_call(
        _mix_out_kernel,
        grid=(t // tm, d // tc),
        in_specs=[
            pl.BlockSpec((tm, d), lambda i, c: (i, 0)),
            pl.BlockSpec((1, d), lambda i, c: (0, 0)),
            pl.BlockSpec((tm, DIFF_WIDTH), lambda i, c: (i, 0)),
            pl.BlockSpec((tm, MOBA_WIDTH), lambda i, c: (i, 0)),
            pl.BlockSpec((d, tc), lambda i, c: (0, ga_blk0 + c)),
            pl.BlockSpec((d, tc), lambda i, c: (0, gb_blk0 + c)),
            pl.BlockSpec((DIFF_WIDTH, tc), lambda i, c: (0, c)),
            pl.BlockSpec((MOBA_WIDTH, tc), lambda i, c: (0, c)),
            pl.BlockSpec((tc, d), lambda i, c: (c, 0)),
        ],
        out_specs=pl.BlockSpec((tm, d), lambda i, c: (i, 0)),
        out_shape=jax.ShapeDtypeStruct((t, d), jnp.float32),
        scratch_shapes=[pltpu.VMEM((tm, d), jnp.bfloat16)],
        compiler_params=_compiler_params(("parallel", "arbitrary")),
        name="mix_out",
    )(x, gain, o_a, o_b, w_gates, w_gates, p_a, p_b, w_o)


LOG2_E = math.log2(math.e)


def _alibi_slopes(n):
    return jnp.asarray(LOG2_E * 2.0 ** (-8.0 * np.arange(1, n + 1) / n), dtype=jnp.float32)


def _qkv_col_scale():
    scale = np.ones((1, QKV_COLS), np.float32)
    scale[:, :DIFF_WIDTH] = LOG2_E * DIFF_HEAD_DIM ** -0.5
    scale[:, 3 * DIFF_WIDTH:3 * DIFF_WIDTH + MOBA_WIDTH] = LOG2_E * MOBA_HEAD_DIM ** -0.5
    return jnp.asarray(scale)


def kernel(x, g_ffn1, w_ffn1_gu, w_ffn1_down, g_mix, w_in, lam_q1, lam_k1, lam_q2, lam_k2, g_subln, p_a, p_b, w_o, g_ffn2, w_ffn2_gu, w_ffn2_down, g_final):
    b, s, d = x.shape
    assert g_ffn1.shape[0] == 1, "single-layer stack"
    bf16 = jnp.bfloat16
    xt = x.reshape(b * s, d)
    g_final_row = g_final.reshape(1, d)

    x1 = _ffn(xt, g_ffn1, w_ffn1_gu[0], w_ffn1_down[0], g_final_row, final_norm=False)

    qkv = _qkv_proj(x1, g_mix, w_in[0], _qkv_col_scale()).reshape(b, s, QKV_COLS)
    o_a = _diff_attn(qkv, _alibi_slopes(N_HEADS_DIFF), lam_q1, lam_k1, lam_q2, lam_k2, g_subln)
    o_b = _moba_attn(qkv, _alibi_slopes(N_HEADS_MOBA))
    x2 = _mix_out(x1, g_mix, o_a.reshape(b * s, DIFF_WIDTH), o_b.reshape(b * s, MOBA_WIDTH),
                  w_in[0][:, QKV_COLS:].astype(bf16), p_a[0].astype(bf16), p_b[0].astype(bf16),
                  w_o[0].astype(bf16))

    out = _ffn(x2, g_ffn2, w_ffn2_gu[0], w_ffn2_down[0], g_final_row, final_norm=True)
    return out.reshape(b, s, d)
```

```python
import functools
import math

import jax
import jax.numpy as jnp
import numpy as np
from jax import lax
from jax.experimental import pallas as pl
from jax.experimental.pallas import tpu as pltpu

D_MODEL = 2048
N_HEADS_DIFF = 8
DIFF_HEAD_DIM = 64
DIFF_WIDTH = N_HEADS_DIFF * 2 * DIFF_HEAD_DIM
N_HEADS_MOBA = 8
MOBA_HEAD_DIM = 128
MOBA_WIDTH = N_HEADS_MOBA * MOBA_HEAD_DIM
MOBA_BLOCK = 256
MOBA_TOPK = 3
D_FF = 5632
RMS_EPS = 1e-6
QKV_COLS = 3 * DIFF_WIDTH + 3 * MOBA_WIDTH
LAM_INIT = 0.8 - 0.6 * math.exp(-0.3 * 0)

LANES = 128
SUBLANES = 8
VMEM_LIMIT_BYTES = 56 * 1024 * 1024
MASK_VALUE = -1e30

FFN_TOKEN_TILE = 1024
FFN_FF_TILE = 256
FFN_EDGE_ROWS = 256
PROJ_TOKEN_TILE = 1024
PROJ_COL_TILE = 1024
PROJ_EDGE_ROWS = 256
MIX_TOKEN_TILE = 512
MIX_COL_TILE = 512
ATTN_TILE = 256

_NT_DIMS = (((1,), (1,)), ((), ()))


def _rms_normalize(x, gain):
    ms = jnp.mean(x * x, axis=-1, keepdims=True)
    return x * lax.rsqrt(ms + RMS_EPS) * gain


def _compiler_params(semantics):
    return pltpu.CompilerParams(dimension_semantics=semantics, vmem_limit_bytes=VMEM_LIMIT_BYTES)


def _ffn_kernel(x_ref, g_ref, wg_ref, wu_ref, wd_ref, gf_ref, o_ref, h_ref, *, n_ff_tiles, final_norm):
    f = pl.program_id(1)
    bf16 = jnp.bfloat16

    def update(row_chunks, first, last):
        weights = (wg_ref[...].astype(bf16), wu_ref[...].astype(bf16), wd_ref[...].astype(bf16))
        for rows in row_chunks:
            update_rows(rows, first, last, *weights)

    def update_rows(rows, first, last, wg, wu, wd):
        if first:
            x = x_ref[rows, :]
            h = _rms_normalize(x, g_ref[...]).astype(bf16)
            h_ref[rows, :] = h
            base = x
        else:
            h = h_ref[rows, :]
            base = o_ref[rows, :]
        gate = jnp.dot(h, wg, preferred_element_type=jnp.float32)
        up = jnp.dot(h, wu, preferred_element_type=jnp.float32)
        act = (gate * jax.nn.sigmoid(gate) * up * 0.5).astype(bf16)
        y = base + jnp.dot(act, wd, preferred_element_type=jnp.float32)
        o_ref[rows, :] = _rms_normalize(y, gf_ref[...]) if (last and final_norm) else y

    tm = x_ref.shape[0]
    chunks = [pl.ds(r, FFN_EDGE_ROWS) for r in range(0, tm, FFN_EDGE_ROWS)]

    whole = [slice(None)]

    @pl.when(f == 0)
    def _():
        update(chunks, True, False)

    if final_norm:
        @pl.when((f > 0) & (f < n_ff_tiles - 1))
        def _():
            update(whole, False, False)

        @pl.when(f == n_ff_tiles - 1)
        def _():
            update(chunks, False, True)
    else:
        @pl.when(f > 0)
        def _():
            update(whole, False, False)


def _ffn(x, gain, w_gu, w_down, final_gain, *, final_norm):
    t, d = x.shape
    d_ff = w_down.shape[0]
    tm, tf = FFN_TOKEN_TILE, FFN_FF_TILE
    n_ff_tiles = d_ff // tf
    assert t % tm == 0 and d_ff % tf == 0 and n_ff_tiles >= 2 and tm % FFN_EDGE_ROWS == 0
    kern = functools.partial(_ffn_kernel, n_ff_tiles=n_ff_tiles, final_norm=final_norm)
    return pl.pallas_call(
        kern,
        grid=(t // tm, n_ff_tiles),
        in_specs=[
            pl.BlockSpec((tm, d), lambda i, f: (i, 0)),
            pl.BlockSpec((1, d), lambda i, f: (0, 0)),
            pl.BlockSpec((d, tf), lambda i, f: (0, f)),
            pl.BlockSpec((d, tf), lambda i, f: (0, f + n_ff_tiles)),
            pl.BlockSpec((tf, d), lambda i, f: (f, 0)),
            pl.BlockSpec((1, d), lambda i, f: (0, 0)),
        ],
        out_specs=pl.BlockSpec((tm, d), lambda i, f: (i, 0)),
        out_shape=jax.ShapeDtypeStruct((t, d), jnp.float32),
        scratch_shapes=[pltpu.VMEM((tm, d), jnp.bfloat16)],
        compiler_params=_compiler_params(("parallel", "arbitrary")),
        name="ffn_final" if final_norm else "ffn",
    )(x, gain, w_gu, w_gu, w_down, final_gain)


def _in_proj_kernel(x_ref, g_ref, w_ref, s_ref, o_ref, h_ref, *, n_qkv_tiles):
    j = pl.program_id(1)

    def project(h, w):
        return jnp.dot(h, w, preferred_element_type=jnp.float32)

    @pl.when(j == 0)
    def _():
        w = w_ref[...].astype(jnp.bfloat16)
        for r in range(0, x_ref.shape[0], PROJ_EDGE_ROWS):
            rows = pl.ds(r, PROJ_EDGE_ROWS)
            h = _rms_normalize(x_ref[rows, :], g_ref[...]).astype(jnp.bfloat16)
            h_ref[rows, :] = h
            o_ref[rows, :] = (project(h, w) * s_ref[...]).astype(o_ref.dtype)

    @pl.when((j > 0) & (j < n_qkv_tiles))
    def _():
        w = w_ref[...].astype(jnp.bfloat16)
        o_ref[...] = (project(h_ref[...], w) * s_ref[...]).astype(o_ref.dtype)

    @pl.when(j >= n_qkv_tiles)
    def _():
        w = w_ref[...].astype(jnp.bfloat16)
        o_ref[...] = jax.nn.sigmoid(project(h_ref[...], w)).astype(o_ref.dtype)


def _in_proj(x, gain, w_in, col_scale):
    t, d = x.shape
    n_cols = w_in.shape[1]
    tm, tn = PROJ_TOKEN_TILE, PROJ_COL_TILE
    assert t % tm == 0 and QKV_COLS % tn == 0 and n_cols % tn == 0
    n_qkv_tiles = QKV_COLS // tn
    return pl.pallas_call(
        functools.partial(_in_proj_kernel, n_qkv_tiles=n_qkv_tiles),
        grid=(t // tm, n_cols // tn),
        in_specs=[
            pl.BlockSpec((tm, d), lambda i, j: (i, 0)),
            pl.BlockSpec((1, d), lambda i, j: (0, 0)),
            pl.BlockSpec((d, tn), lambda i, j: (0, j)),
            pl.BlockSpec((1, tn), lambda i, j: (0, jnp.minimum(j, n_qkv_tiles - 1))),
        ],
        out_specs=pl.BlockSpec((tm, tn), lambda i, j: (i, j)),
        out_shape=jax.ShapeDtypeStruct((t, n_cols), jnp.bfloat16),
        scratch_shapes=[pltpu.VMEM((tm, d), jnp.bfloat16)],
        compiler_params=_compiler_params(("parallel", "arbitrary")),
        name="in_proj",
    )(x, gain, w_in, col_scale)


ALIBI_TERMS = 3


def _alibi_columns(slope, t):
    lane = lax.broadcasted_iota(jnp.int32, (t, LANES), 1)
    rest = lax.broadcasted_iota(jnp.int32, (t, LANES), 0).astype(jnp.float32) * slope
    k_extra = jnp.zeros((t, LANES), jnp.float32)
    for i in range(ALIBI_TERMS):
        term = rest.astype(jnp.bfloat16).astype(jnp.float32)
        k_extra = jnp.where(lane == i, term, k_extra)
        rest = rest - term
    q_extra = jnp.where(lane < ALIBI_TERMS, 1.0, 0.0)
    return k_extra.astype(jnp.bfloat16), q_extra.astype(jnp.bfloat16)


def _query_shift(slope, t):
    return lax.broadcasted_iota(jnp.int32, (1, t), 1).astype(jnp.float32) * (-slope)


def _causal_mask_t(t):
    kk = lax.broadcasted_iota(jnp.int32, (t, t), 0)
    qq = lax.broadcasted_iota(jnp.int32, (t, t), 1)
    return jnp.where(qq >= kk, 0.0, MASK_VALUE)


def _fold_rows(x, op):
    rows, cols = x.shape
    return op(x.reshape(rows // SUBLANES, SUBLANES, cols), axis=0)


def _score_tiles(k_tile, q, shifts, mask_diag, s_ref):
    n, t = len(shifts), q.shape[0]
    cand = None
    for j in range(n):
        sj = lax.dot_general(k_tile(j), q, _NT_DIMS, preferred_element_type=jnp.float32)
        if j == n - 1:
            sj = sj + mask_diag
        s_ref[j * t:(j + 1) * t, :] = sj
        cj = _fold_rows(sj, jnp.max) + shifts[j]
        cand = cj if cand is None else jnp.maximum(cand, cj)
    return jnp.max(cand, axis=0, keepdims=True)


def _softmax_pv(s_ref, m, shifts, vt_tile):
    t = m.shape[1]
    lpart, acc = None, None
    for j in range(len(shifts)):
        sj = s_ref[j * t:(j + 1) * t, :]
        p = jnp.exp2(sj - (m - shifts[j]))
        pj = _fold_rows(p, jnp.sum)
        lpart = pj if lpart is None else lpart + pj
        vt = vt_tile(j)
        pv = jnp.dot(vt, p.astype(vt.dtype), preferred_element_type=jnp.float32)
        acc = pv if acc is None else acc + pv
    return acc / jnp.sum(lpart, axis=0, keepdims=True)


def _run_pipelined(stage_a, stage_b, n):
    pending = stage_a(0)
    for u in range(n):
        nxt = stage_a(u + 1) if u + 1 < n else None
        stage_b(u, pending)
        pending = nxt


def _diff_attn_kernel(slopes_ref, lq1_ref, lk1_ref, lq2_ref, lk2_ref, gs_ref,
                      q_ref, k_ref, v_ref, o_ref, vt_ref, s_ref):
    slope = slopes_ref[pl.program_id(1)]
    t = ATTN_TILE
    n_tiles = q_ref.shape[1] // t

    vt_ref[...] = v_ref[0].T
    k_extra, q_extra = _alibi_columns(slope, t)
    mask_diag = _causal_mask_t(t)
    q_shift = _query_shift(slope, t)
    lam = (jnp.exp(jnp.sum(lq1_ref[...] * lk1_ref[...], axis=-1, keepdims=True))
           - jnp.exp(jnp.sum(lq2_ref[...] * lk2_ref[...], axis=-1, keepdims=True))
           + LAM_INIT)
    lane = lax.broadcasted_iota(jnp.int32, (t, LANES), 1)
    k_tile = lambda j: jnp.concatenate([k_ref[0, j * t:(j + 1) * t, :], k_extra], axis=1)
    vt_tile = lambda j: vt_ref[:, j * t:(j + 1) * t]
    shifts = lambda c: [q_shift - slope * float((c - j) * t) for j in range(c + 1)]

    def scores(c):
        q = q_ref[0, c * t:(c + 1) * t, :]
        zero = jnp.zeros_like(q)
        q_maps = (jnp.where(lane < DIFF_HEAD_DIM, q, zero), jnp.where(lane >= DIFF_HEAD_DIM, q, zero))
        return [_score_tiles(k_tile, jnp.concatenate([qm, q_extra], axis=1), shifts(c), mask_diag,
                             s_ref.at[c % 2, mi])
                for mi, qm in enumerate(q_maps)]

    def finish(c, maxes):
        o0, o1 = [_softmax_pv(s_ref.at[c % 2, mi], m, shifts(c), vt_tile) for mi, m in enumerate(maxes)]
        o = (o0 - lam * o1).T
        o_ref[0, c * t:(c + 1) * t, :] = (
            _rms_normalize(o, gs_ref[...]) * (1.0 - LAM_INIT)).astype(o_ref.dtype)

    _run_pipelined(scores, finish, n_tiles)


def _diff_attn(qkv, slopes, lam_q1, lam_k1, lam_q2, lam_k2, g_subln):
    b, s, _ = qkv.shape
    nh, hw = N_HEADS_DIFF, 2 * DIFF_HEAD_DIM
    assert hw == LANES and s % ATTN_TILE == 0
    k_blk0, v_blk0 = DIFF_WIDTH // hw, 2 * DIFF_WIDTH // hw
    lam_spec = pl.BlockSpec((1, DIFF_HEAD_DIM), lambda bi, h: (0, 0))
    return pl.pallas_call(
        _diff_attn_kernel,
        grid=(b, nh),
        in_specs=[
            pl.BlockSpec(memory_space=pltpu.SMEM),
            lam_spec, lam_spec, lam_spec, lam_spec,
            pl.BlockSpec((1, hw), lambda bi, h: (0, 0)),
            pl.BlockSpec((1, s, hw), lambda bi, h: (bi, 0, h)),
            pl.BlockSpec((1, s, hw), lambda bi, h: (bi, 0, k_blk0 + h)),
            pl.BlockSpec((1, s, hw), lambda bi, h: (bi, 0, v_blk0 + h)),
        ],
        out_specs=pl.BlockSpec((1, s, hw), lambda bi, h: (bi, 0, h)),
        out_shape=jax.ShapeDtypeStruct((b, s, DIFF_WIDTH), jnp.bfloat16),
        scratch_shapes=[pltpu.VMEM((hw, s), jnp.bfloat16),
                        pltpu.VMEM((2, 2, s, ATTN_TILE), jnp.float32)],
        compiler_params=_compiler_params(("parallel", "parallel")),
        name="diff_attn",
    )(slopes, lam_q1, lam_k1, lam_q2, lam_k2, g_subln, qkv, qkv, qkv)


def _moba_attn_kernel(slopes_ref, q_ref, k_ref, v_ref, o_ref, vt_ref, s_ref, *, n_blocks, n_sel):
    slope = slopes_ref[pl.program_id(1)]
    blk, dh = MOBA_BLOCK, MOBA_HEAD_DIM

    vt_ref[...] = v_ref[0].T
    k_extra, q_extra = _alibi_columns(slope, blk)
    mask_diag = _causal_mask_t(blk)
    q_shift = _query_shift(slope, blk)

    kmean = jnp.mean(k_ref[0].astype(jnp.float32).reshape(n_blocks, blk, dh), axis=1)
    kmean = jnp.concatenate([kmean, jnp.zeros((LANES - n_blocks, dh), jnp.float32)], axis=0)
    kmean_hi = kmean.astype(jnp.bfloat16)
    kmean_lo = (kmean - kmean_hi.astype(jnp.float32)).astype(jnp.bfloat16)
    blk_id = lax.broadcasted_iota(jnp.int32, (n_blocks, blk), 0)
    k_tile = lambda j: jnp.concatenate([k_ref[0, j * blk:(j + 1) * blk, :], k_extra], axis=1)
    vt_tile = lambda j: vt_ref[:, j * blk:(j + 1) * blk]

    def scores(c):
        q = q_ref[0, c * blk:(c + 1) * blk, :]
        g = (lax.dot_general(kmean_hi, q, _NT_DIMS, preferred_element_type=jnp.float32)
             + lax.dot_general(kmean_lo, q, _NT_DIMS, preferred_element_type=jnp.float32))[:n_blocks]
        past = blk_id < c
        g = jnp.where(past, g, -jnp.inf)
        rank = jnp.zeros(g.shape, jnp.int32)
        for r in range(1, n_blocks):
            other = pltpu.roll(g, r, 0)
            rank += jnp.where(blk_id >= r, (other >= g).astype(jnp.int32), (other > g).astype(jnp.int32))
        keep = (past & (rank < n_sel)) | (blk_id == c)
        sel_bias = jnp.where(keep, 0.0, MASK_VALUE)
        shifts = [sel_bias[j:j + 1, :] + (q_shift - slope * float((c - j) * blk)) for j in range(c + 1)]
        q_aug = jnp.concatenate([q, q_extra], axis=1)
        return _score_tiles(k_tile, q_aug, shifts, mask_diag, s_ref.at[c % 2]), shifts

    def finish(c, state):
        m, shifts = state
        o = _softmax_pv(s_ref.at[c % 2], m, shifts, vt_tile)
        o_ref[0, c * blk:(c + 1) * blk, :] = o.T.astype(o_ref.dtype)

    _run_pipelined(scores, finish, n_blocks)


def _moba_attn(qkv, slopes):
    b, s, _ = qkv.shape
    nh, dh, blk = N_HEADS_MOBA, MOBA_HEAD_DIM, MOBA_BLOCK
    assert dh == LANES and s % blk == 0
    n_blocks = s // blk
    assert n_blocks == SUBLANES, "block ranking uses one vreg row per MoBA block"
    n_sel = min(MOBA_TOPK, n_blocks - 1)
    q_blk0 = 3 * DIFF_WIDTH // dh
    k_blk0, v_blk0 = q_blk0 + nh, q_blk0 + 2 * nh
    kern = functools.partial(_moba_attn_kernel, n_blocks=n_blocks, n_sel=n_sel)
    return pl.pallas_call(
        kern,
        grid=(b, nh),
        in_specs=[
            pl.BlockSpec(memory_space=pltpu.SMEM),
            pl.BlockSpec((1, s, dh), lambda bi, h: (bi, 0, q_blk0 + h)),
            pl.BlockSpec((1, s, dh), lambda bi, h: (bi, 0, k_blk0 + h)),
            pl.BlockSpec((1, s, dh), lambda bi, h: (bi, 0, v_blk0 + h)),
        ],
        out_specs=pl.BlockSpec((1, s, dh), lambda bi, h: (bi, 0, h)),
        out_shape=jax.ShapeDtypeStruct((b, s, MOBA_WIDTH), jnp.bfloat16),
        scratch_shapes=[pltpu.VMEM((dh, s), jnp.bfloat16),
                        pltpu.VMEM((2, s, blk), jnp.float32)],
        compiler_params=_compiler_params(("parallel", "parallel")),
        name="moba_attn",
    )(slopes, qkv, qkv, qkv)


def _mix_out_kernel(x_ref, oa_ref, ob_ref, sa_ref, sb_ref, pa_ref, pb_ref, wo_ref, o_ref):
    @pl.when(pl.program_id(1) == 0)
    def _():
        o_ref[...] = x_ref[...]

    bf16 = jnp.bfloat16
    proj_a = jnp.dot(oa_ref[...], pa_ref[...].astype(bf16), preferred_element_type=jnp.float32)
    proj_b = jnp.dot(ob_ref[...], pb_ref[...].astype(bf16), preferred_element_type=jnp.float32)
    merged = sa_ref[...].astype(jnp.float32) * proj_a + sb_ref[...].astype(jnp.float32) * proj_b
    o_ref[...] += jnp.dot(merged.astype(bf16), wo_ref[...].astype(bf16), preferred_element_type=jnp.float32)


def _mix_out(x, o_a, o_b, proj, p_a, p_b, w_o):
    t, d = x.shape
    tm, tc = MIX_TOKEN_TILE, MIX_COL_TILE
    assert t % tm == 0 and d % tc == 0 and QKV_COLS % tc == 0
    sa_blk0 = QKV_COLS // tc
    sb_blk0 = sa_blk0 + d // tc
    return pl.pallas_call(
        _mix_out_kernel,
        grid=(t // tm, d // tc),
        in_specs=[
            pl.BlockSpec((tm, d), lambda i, c: (i, 0)),
            pl.BlockSpec((tm, DIFF_WIDTH), lambda i, c: (i, 0)),
            pl.BlockSpec((tm, MOBA_WIDTH), lambda i, c: (i, 0)),
            pl.BlockSpec((tm, tc), lambda i, c: (i, sa_blk0 + c)),
            pl.BlockSpec((tm, tc), lambda i, c: (i, sb_blk0 + c)),
            pl.BlockSpec((DIFF_WIDTH, tc), lambda i, c: (0, c)),
            pl.BlockSpec((MOBA_WIDTH, tc), lambda i, c: (0, c)),
            pl.BlockSpec((tc, d), lambda i, c: (c, 0)),
        ],
        out_specs=pl.BlockSpec((tm, d), lambda i, c: (i, 0)),
        out_shape=jax.ShapeDtypeStruct((t, d), jnp.float32),
        compiler_params=_compiler_params(("parallel", "arbitrary")),
        name="mix_out",
    )(x, o_a, o_b, proj, proj, p_a, p_b, w_o)


LOG2_E = math.log2(math.e)


def _alibi_slopes(n):
    return jnp.asarray(LOG2_E * 2.0 ** (-8.0 * np.arange(1, n + 1) / n), dtype=jnp.float32)


def _qkv_col_scale():
    scale = np.ones((1, QKV_COLS), np.float32)
    scale[:, :DIFF_WIDTH] = LOG2_E * DIFF_HEAD_DIM ** -0.5
    scale[:, 3 * DIFF_WIDTH:3 * DIFF_WIDTH + MOBA_WIDTH] = LOG2_E * MOBA_HEAD_DIM ** -0.5
    return jnp.asarray(scale)


def kernel(x, g_ffn1, w_ffn1_gu, w_ffn1_down, g_mix, w_in, lam_q1, lam_k1, lam_q2, lam_k2, g_subln, p_a, p_b, w_o, g_ffn2, w_ffn2_gu, w_ffn2_down, g_final):
    b, s, d = x.shape
    assert g_ffn1.shape[0] == 1, "single-layer stack"
    xt = x.reshape(b * s, d)
    g_final_row = g_final.reshape(1, d)

    x1 = _ffn(xt, g_ffn1, w_ffn1_gu[0], w_ffn1_down[0], g_final_row, final_norm=False)

    proj = _in_proj(x1, g_mix, w_in[0], _qkv_col_scale())
    qkv = proj.reshape(b, s, proj.shape[1])
    o_a = _diff_attn(qkv, _alibi_slopes(N_HEADS_DIFF), lam_q1, lam_k1, lam_q2, lam_k2, g_subln)
    o_b = _moba_attn(qkv, _alibi_slopes(N_HEADS_MOBA))
    x2 = _mix_out(x1, o_a.reshape(b * s, DIFF_WIDTH), o_b.reshape(b * s, MOBA_WIDTH), proj,
                  p_a[0], p_b[0], w_o[0])

    out = _ffn(x2, g_ffn2, w_ffn2_gu[0], w_ffn2_down[0], g_final_row, final_norm=True)
    return out.reshape(b, s, d)
```

```python
import functools
import math

import jax
import jax.numpy as jnp
import numpy as np
from jax import lax
from jax.experimental import pallas as pl
from jax.experimental.pallas import tpu as pltpu

D_MODEL = 2048
N_HEADS_DIFF = 8
DIFF_HEAD_DIM = 64
DIFF_WIDTH = N_HEADS_DIFF * 2 * DIFF_HEAD_DIM
N_HEADS_MOBA = 8
MOBA_HEAD_DIM = 128
MOBA_WIDTH = N_HEADS_MOBA * MOBA_HEAD_DIM
MOBA_BLOCK = 256
MOBA_TOPK = 3
D_FF = 5632
RMS_EPS = 1e-6
QKV_COLS = 3 * DIFF_WIDTH + 3 * MOBA_WIDTH
LAM_INIT = 0.8 - 0.6 * math.exp(-0.3 * 0)

LANES = 128
SUBLANES = 8
VMEM_LIMIT_BYTES = 56 * 1024 * 1024
MASK_VALUE = -1e30

FFN_TOKEN_TILE = 1024
FFN_FF_TILE = 256
PROJ_TOKEN_TILE = 1024
PROJ_COL_TILE = 1024
MIX_TOKEN_TILE = 1024
MIX_COL_TILE = 256
ATTN_TILE = 256

_NT_DIMS = (((1,), (1,)), ((), ()))


def _rms_normalize(x, gain):
    ms = jnp.mean(x * x, axis=-1, keepdims=True)
    return x * lax.rsqrt(ms + RMS_EPS) * gain


def _compiler_params(semantics):
    return pltpu.CompilerParams(dimension_semantics=semantics, vmem_limit_bytes=VMEM_LIMIT_BYTES)


def _ffn_kernel(x_ref, g_ref, wg_ref, wu_ref, wd_ref, gf_ref, o_ref, h_ref, *, n_ff_tiles, final_norm):
    f = pl.program_id(1)
    bf16 = jnp.bfloat16

    @pl.when(f == 0)
    def _():
        x = x_ref[...]
        h_ref[...] = _rms_normalize(x, g_ref[...]).astype(bf16)
        o_ref[...] = x

    h = h_ref[...]
    gate = jnp.dot(h, wg_ref[...].astype(bf16), preferred_element_type=jnp.float32)
    up = jnp.dot(h, wu_ref[...].astype(bf16), preferred_element_type=jnp.float32)
    act = (gate * jax.nn.sigmoid(gate) * up * 0.5).astype(bf16)
    o_ref[...] += jnp.dot(act, wd_ref[...].astype(bf16), preferred_element_type=jnp.float32)

    if final_norm:
        @pl.when(f == n_ff_tiles - 1)
        def _():
            o_ref[...] = _rms_normalize(o_ref[...], gf_ref[...])


def _ffn(x, gain, w_gu, w_down, final_gain, *, final_norm):
    t, d = x.shape
    d_ff = w_down.shape[0]
    tm, tf = FFN_TOKEN_TILE, FFN_FF_TILE
    n_ff_tiles = d_ff // tf
    assert t % tm == 0 and d_ff % tf == 0
    kern = functools.partial(_ffn_kernel, n_ff_tiles=n_ff_tiles, final_norm=final_norm)
    return pl.pallas_call(
        kern,
        grid=(t // tm, n_ff_tiles),
        in_specs=[
            pl.BlockSpec((tm, d), lambda i, f: (i, 0)),
            pl.BlockSpec((1, d), lambda i, f: (0, 0)),
            pl.BlockSpec((d, tf), lambda i, f: (0, f)),
            pl.BlockSpec((d, tf), lambda i, f: (0, f + n_ff_tiles)),
            pl.BlockSpec((tf, d), lambda i, f: (f, 0)),
            pl.BlockSpec((1, d), lambda i, f: (0, 0)),
        ],
        out_specs=pl.BlockSpec((tm, d), lambda i, f: (i, 0)),
        out_shape=jax.ShapeDtypeStruct((t, d), jnp.float32),
        scratch_shapes=[pltpu.VMEM((tm, d), jnp.bfloat16)],
        compiler_params=_compiler_params(("parallel", "arbitrary")),
        name="ffn_final" if final_norm else "ffn",
    )(x, gain, w_gu, w_gu, w_down, final_gain)


def _in_proj_kernel(x_ref, g_ref, w_ref, s_ref, o_ref, h_ref, *, n_qkv_tiles):
    j = pl.program_id(1)

    @pl.when(j == 0)
    def _():
        h_ref[...] = _rms_normalize(x_ref[...], g_ref[...]).astype(jnp.bfloat16)

    def project():
        return jnp.dot(h_ref[...], w_ref[...].astype(jnp.bfloat16), preferred_element_type=jnp.float32)

    @pl.when(j < n_qkv_tiles)
    def _():
        o_ref[...] = (project() * s_ref[...]).astype(o_ref.dtype)

    @pl.when(j >= n_qkv_tiles)
    def _():
        o_ref[...] = jax.nn.sigmoid(project()).astype(o_ref.dtype)


def _in_proj(x, gain, w_in, col_scale):
    t, d = x.shape
    n_cols = w_in.shape[1]
    tm, tn = PROJ_TOKEN_TILE, PROJ_COL_TILE
    assert t % tm == 0 and QKV_COLS % tn == 0 and n_cols % tn == 0
    n_qkv_tiles = QKV_COLS // tn
    return pl.pallas_call(
        functools.partial(_in_proj_kernel, n_qkv_tiles=n_qkv_tiles),
        grid=(t // tm, n_cols // tn),
        in_specs=[
            pl.BlockSpec((tm, d), lambda i, j: (i, 0)),
            pl.BlockSpec((1, d), lambda i, j: (0, 0)),
            pl.BlockSpec((d, tn), lambda i, j: (0, j)),
            pl.BlockSpec((1, tn), lambda i, j: (0, jnp.minimum(j, n_qkv_tiles - 1))),
        ],
        out_specs=pl.BlockSpec((tm, tn), lambda i, j: (i, j)),
        out_shape=jax.ShapeDtypeStruct((t, n_cols), jnp.bfloat16),
        scratch_shapes=[pltpu.VMEM((tm, d), jnp.bfloat16)],
        compiler_params=_compiler_params(("parallel", "arbitrary")),
        name="in_proj",
    )(x, gain, w_in, col_scale)


ALIBI_TERMS = 3


def _alibi_columns(slope, t):
    lane = lax.broadcasted_iota(jnp.int32, (t, LANES), 1)
    rest = lax.broadcasted_iota(jnp.int32, (t, LANES), 0).astype(jnp.float32) * slope
    k_extra = jnp.zeros((t, LANES), jnp.float32)
    for i in range(ALIBI_TERMS):
        term = rest.astype(jnp.bfloat16).astype(jnp.float32)
        k_extra = jnp.where(lane == i, term, k_extra)
        rest = rest - term
    q_extra = jnp.where(lane < ALIBI_TERMS, 1.0, 0.0)
    return k_extra.astype(jnp.bfloat16), q_extra.astype(jnp.bfloat16)


def _query_shift(slope, t):
    return lax.broadcasted_iota(jnp.int32, (1, t), 1).astype(jnp.float32) * (-slope)


def _causal_mask_t(t):
    kk = lax.broadcasted_iota(jnp.int32, (t, t), 0)
    qq = lax.broadcasted_iota(jnp.int32, (t, t), 1)
    return jnp.where(qq >= kk, 0.0, MASK_VALUE)


def _fold_rows(x, op):
    rows, cols = x.shape
    return op(x.reshape(rows // SUBLANES, SUBLANES, cols), axis=0)


def _score_tiles(k_tile, q, shifts, mask_diag, s_ref):
    n, t = len(shifts), q.shape[0]
    cand = None
    for j in range(n):
        sj = lax.dot_general(k_tile(j), q, _NT_DIMS, preferred_element_type=jnp.float32)
        if j == n - 1:
            sj = sj + mask_diag
        s_ref[j * t:(j + 1) * t, :] = sj
        cj = _fold_rows(sj, jnp.max) + shifts[j]
        cand = cj if cand is None else jnp.maximum(cand, cj)
    return jnp.max(cand, axis=0, keepdims=True)


def _softmax_pv(s_ref, m, shifts, vt_tile):
    t = m.shape[1]
    lpart, acc = None, None
    for j in range(len(shifts)):
        sj = s_ref[j * t:(j + 1) * t, :]
        p = jnp.exp2(sj - (m - shifts[j]))
        pj = _fold_rows(p, jnp.sum)
        lpart = pj if lpart is None else lpart + pj
        vt = vt_tile(j)
        pv = jnp.dot(vt, p.astype(vt.dtype), preferred_element_type=jnp.float32)
        acc = pv if acc is None else acc + pv
    return acc / jnp.sum(lpart, axis=0, keepdims=True)


def _run_pipelined(stage_a, stage_b, n):
    pending = stage_a(0)
    for u in range(n):
        nxt = stage_a(u + 1) if u + 1 < n else None
        stage_b(u, pending)
        pending = nxt


def _diff_attn_kernel(slopes_ref, lq1_ref, lk1_ref, lq2_ref, lk2_ref, gs_ref,
                      q_ref, k_ref, v_ref, o_ref, vt_ref, s_ref):
    slope = slopes_ref[pl.program_id(1)]
    t = ATTN_TILE
    n_tiles = q_ref.shape[1] // t

    vt_ref[...] = v_ref[0].T
    k_extra, q_extra = _alibi_columns(slope, t)
    mask_diag = _causal_mask_t(t)
    q_shift = _query_shift(slope, t)
    lam = (jnp.exp(jnp.sum(lq1_ref[...] * lk1_ref[...], axis=-1, keepdims=True))
           - jnp.exp(jnp.sum(lq2_ref[...] * lk2_ref[...], axis=-1, keepdims=True))
           + LAM_INIT)
    lane = lax.broadcasted_iota(jnp.int32, (t, LANES), 1)
    k_tile = lambda j: jnp.concatenate([k_ref[0, j * t:(j + 1) * t, :], k_extra], axis=1)
    vt_tile = lambda j: vt_ref[:, j * t:(j + 1) * t]
    shifts = lambda c: [q_shift - slope * float((c - j) * t) for j in range(c + 1)]

    def scores(c):
        q = q_ref[0, c * t:(c + 1) * t, :]
        zero = jnp.zeros_like(q)
        q_maps = (jnp.where(lane < DIFF_HEAD_DIM, q, zero), jnp.where(lane >= DIFF_HEAD_DIM, q, zero))
        return [_score_tiles(k_tile, jnp.concatenate([qm, q_extra], axis=1), shifts(c), mask_diag,
                             s_ref.at[c % 2, mi])
                for mi, qm in enumerate(q_maps)]

    def finish(c, maxes):
        o0, o1 = [_softmax_pv(s_ref.at[c % 2, mi], m, shifts(c), vt_tile) for mi, m in enumerate(maxes)]
        o = (o0 - lam * o1).T
        o_ref[0, c * t:(c + 1) * t, :] = (
            _rms_normalize(o, gs_ref[...]) * (1.0 - LAM_INIT)).astype(o_ref.dtype)

    _run_pipelined(scores, finish, n_tiles)


def _diff_attn(qkv, slopes, lam_q1, lam_k1, lam_q2, lam_k2, g_subln):
    b, s, _ = qkv.shape
    nh, hw = N_HEADS_DIFF, 2 * DIFF_HEAD_DIM
    assert hw == LANES and s % ATTN_TILE == 0
    k_blk0, v_blk0 = DIFF_WIDTH // hw, 2 * DIFF_WIDTH // hw
    lam_spec = pl.BlockSpec((1, DIFF_HEAD_DIM), lambda bi, h: (0, 0))
    return pl.pallas_call(
        _diff_attn_kernel,
        grid=(b, nh),
        in_specs=[
            pl.BlockSpec(memory_space=pltpu.SMEM),
            lam_spec, lam_spec, lam_spec, lam_spec,
            pl.BlockSpec((1, hw), lambda bi, h: (0, 0)),
            pl.BlockSpec((1, s, hw), lambda bi, h: (bi, 0, h)),
            pl.BlockSpec((1, s, hw), lambda bi, h: (bi, 0, k_blk0 + h)),
            pl.BlockSpec((1, s, hw), lambda bi, h: (bi, 0, v_blk0 + h)),
        ],
        out_specs=pl.BlockSpec((1, s, hw), lambda bi, h: (bi, 0, h)),
        out_shape=jax.ShapeDtypeStruct((b, s, DIFF_WIDTH), jnp.bfloat16),
        scratch_shapes=[pltpu.VMEM((hw, s), jnp.bfloat16),
                        pltpu.VMEM((2, 2, s, ATTN_TILE), jnp.float32)],
        compiler_params=_compiler_params(("parallel", "parallel")),
        name="diff_attn",
    )(slopes, lam_q1, lam_k1, lam_q2, lam_k2, g_subln, qkv, qkv, qkv)


def _moba_attn_kernel(slopes_ref, q_ref, k_ref, v_ref, o_ref, vt_ref, s_ref, *, n_blocks, n_sel):
    slope = slopes_ref[pl.program_id(1)]
    blk, dh = MOBA_BLOCK, MOBA_HEAD_DIM

    vt_ref[...] = v_ref[0].T
    k_extra, q_extra = _alibi_columns(slope, blk)
    mask_diag = _causal_mask_t(blk)
    q_shift = _query_shift(slope, blk)

    kmean = jnp.mean(k_ref[0].astype(jnp.float32).reshape(n_blocks, blk, dh), axis=1)
    kmean = jnp.concatenate([kmean, jnp.zeros((LANES - n_blocks, dh), jnp.float32)], axis=0)
    kmean_hi = kmean.astype(jnp.bfloat16)
    kmean_lo = (kmean - kmean_hi.astype(jnp.float32)).astype(jnp.bfloat16)
    blk_id = lax.broadcasted_iota(jnp.int32, (n_blocks, blk), 0)
    k_tile = lambda j: jnp.concatenate([k_ref[0, j * blk:(j + 1) * blk, :], k_extra], axis=1)
    vt_tile = lambda j: vt_ref[:, j * blk:(j + 1) * blk]

    def scores(c):
        q = q_ref[0, c * blk:(c + 1) * blk, :]
        g = (lax.dot_general(kmean_hi, q, _NT_DIMS, preferred_element_type=jnp.float32)
             + lax.dot_general(kmean_lo, q, _NT_DIMS, preferred_element_type=jnp.float32))[:n_blocks]
        past = blk_id < c
        g = jnp.where(past, g, -jnp.inf)
        rank = jnp.zeros(g.shape, jnp.int32)
        for r in range(1, n_blocks):
            other = pltpu.roll(g, r, 0)
            rank += jnp.where(blk_id >= r, (other >= g).astype(jnp.int32), (other > g).astype(jnp.int32))
        keep = (past & (rank < n_sel)) | (blk_id == c)
        sel_bias = jnp.where(keep, 0.0, MASK_VALUE)
        shifts = [sel_bias[j:j + 1, :] + (q_shift - slope * float((c - j) * blk)) for j in range(c + 1)]
        q_aug = jnp.concatenate([q, q_extra], axis=1)
        return _score_tiles(k_tile, q_aug, shifts, mask_diag, s_ref.at[c % 2]), shifts

    def finish(c, state):
        m, shifts = state
        o = _softmax_pv(s_ref.at[c % 2], m, shifts, vt_tile)
        o_ref[0, c * blk:(c + 1) * blk, :] = o.T.astype(o_ref.dtype)

    _run_pipelined(scores, finish, n_blocks)


def _moba_attn(qkv, slopes):
    b, s, _ = qkv.shape
    nh, dh, blk = N_HEADS_MOBA, MOBA_HEAD_DIM, MOBA_BLOCK
    assert dh == LANES and s % blk == 0
    n_blocks = s // blk
    assert n_blocks == SUBLANES, "block ranking uses one vreg row per MoBA block"
    n_sel = min(MOBA_TOPK, n_blocks - 1)
    q_blk0 = 3 * DIFF_WIDTH // dh
    k_blk0, v_blk0 = q_blk0 + nh, q_blk0 + 2 * nh
    kern = functools.partial(_moba_attn_kernel, n_blocks=n_blocks, n_sel=n_sel)
    return pl.pallas_call(
        kern,
        grid=(b, nh),
        in_specs=[
            pl.BlockSpec(memory_space=pltpu.SMEM),
            pl.BlockSpec((1, s, dh), lambda bi, h: (bi, 0, q_blk0 + h)),
            pl.BlockSpec((1, s, dh), lambda bi, h: (bi, 0, k_blk0 + h)),
            pl.BlockSpec((1, s, dh), lambda bi, h: (bi, 0, v_blk0 + h)),
        ],
        out_specs=pl.BlockSpec((1, s, dh), lambda bi, h: (bi, 0, h)),
        out_shape=jax.ShapeDtypeStruct((b, s, MOBA_WIDTH), jnp.bfloat16),
        scratch_shapes=[pltpu.VMEM((dh, s), jnp.bfloat16),
                        pltpu.VMEM((2, s, blk), jnp.float32)],
        compiler_params=_compiler_params(("parallel", "parallel")),
        name="moba_attn",
    )(slopes, qkv, qkv, qkv)


def _mix_out_kernel(x_ref, oa_ref, ob_ref, sa_ref, sb_ref, pa_ref, pb_ref, wo_ref, o_ref):
    @pl.when(pl.program_id(1) == 0)
    def _():
        o_ref[...] = x_ref[...]

    bf16 = jnp.bfloat16
    proj_a = jnp.dot(oa_ref[...], pa_ref[...].astype(bf16), preferred_element_type=jnp.float32)
    proj_b = jnp.dot(ob_ref[...], pb_ref[...].astype(bf16), preferred_element_type=jnp.float32)
    merged = sa_ref[...].astype(jnp.float32) * proj_a + sb_ref[...].astype(jnp.float32) * proj_b
    o_ref[...] += jnp.dot(merged.astype(bf16), wo_ref[...].astype(bf16), preferred_element_type=jnp.float32)


def _mix_out(x, o_a, o_b, proj, p_a, p_b, w_o):
    t, d = x.shape
    tm, tc = MIX_TOKEN_TILE, MIX_COL_TILE
    assert t % tm == 0 and d % tc == 0 and QKV_COLS % tc == 0
    sa_blk0 = QKV_COLS // tc
    sb_blk0 = sa_blk0 + d // tc
    return pl.pallas_call(
        _mix_out_kernel,
        grid=(t // tm, d // tc),
        in_specs=[
            pl.BlockSpec((tm, d), lambda i, c: (i, 0)),
            pl.BlockSpec((tm, DIFF_WIDTH), lambda i, c: (i, 0)),
            pl.BlockSpec((tm, MOBA_WIDTH), lambda i, c: (i, 0)),
            pl.BlockSpec((tm, tc), lambda i, c: (i, sa_blk0 + c)),
            pl.BlockSpec((tm, tc), lambda i, c: (i, sb_blk0 + c)),
            pl.BlockSpec((DIFF_WIDTH, tc), lambda i, c: (0, c)),
            pl.BlockSpec((MOBA_WIDTH, tc), lambda i, c: (0, c)),
            pl.BlockSpec((tc, d), lambda i, c: (c, 0)),
        ],
        out_specs=pl.BlockSpec((tm, d), lambda i, c: (i, 0)),
        out_shape=jax.ShapeDtypeStruct((t, d), jnp.float32),
        compiler_params=_compiler_params(("parallel", "arbitrary")),
        name="mix_out",
    )(x, o_a, o_b, proj, proj, p_a, p_b, w_o)


LOG2_E = math.log2(math.e)


def _alibi_slopes(n):
    return jnp.asarray(LOG2_E * 2.0 ** (-8.0 * np.arange(1, n + 1) / n), dtype=jnp.float32)


def _qkv_col_scale():
    scale = np.ones((1, QKV_COLS), np.float32)
    scale[:, :DIFF_WIDTH] = LOG2_E * DIFF_HEAD_DIM ** -0.5
    scale[:, 3 * DIFF_WIDTH:3 * DIFF_WIDTH + MOBA_WIDTH] = LOG2_E * MOBA_HEAD_DIM ** -0.5
    return jnp.asarray(scale)


def kernel(x, g_ffn1, w_ffn1_gu, w_ffn1_down, g_mix, w_in, lam_q1, lam_k1, lam_q2, lam_k2, g_subln, p_a, p_b, w_o, g_ffn2, w_ffn2_gu, w_ffn2_down, g_final):
    b, s, d = x.shape
    assert g_ffn1.shape[0] == 1, "single-layer stack"
    xt = x.reshape(b * s, d)
    g_final_row = g_final.reshape(1, d)

    x1 = _ffn(xt, g_ffn1, w_ffn1_gu[0], w_ffn1_down[0], g_final_row, final_norm=False)

    proj = _in_proj(x1, g_mix, w_in[0], _qkv_col_scale())
    qkv = proj.reshape(b, s, proj.shape[1])
    o_a = _diff_attn(qkv, _alibi_slopes(N_HEADS_DIFF), lam_q1, lam_k1, lam_q2, lam_k2, g_subln)
    o_b = _moba_attn(qkv, _alibi_slopes(N_HEADS_MOBA))
    x2 = _mix_out(x1, o_a.reshape(b * s, DIFF_WIDTH), o_b.reshape(b * s, MOBA_WIDTH), proj,
                  p_a[0], p_b[0], w_o[0])

    out = _ffn(x2, g_ffn2, w_ffn2_gu[0], w_ffn2_down[0], g_final_row, final_norm=True)
    return out.reshape(b, s, d)
```

```python
import functools
import math

import jax
import jax.numpy as jnp
import numpy as np
from jax import lax
from jax.experimental import pallas as pl
from jax.experimental.pallas import tpu as pltpu

D_MODEL = 2048
N_HEADS_DIFF = 8
DIFF_HEAD_DIM = 64
DIFF_WIDTH = N_HEADS_DIFF * 2 * DIFF_HEAD_DIM
N_HEADS_MOBA = 8
MOBA_HEAD_DIM = 128
MOBA_WIDTH = N_HEADS_MOBA * MOBA_HEAD_DIM
MOBA_BLOCK = 256
MOBA_TOPK = 3
D_FF = 5632
RMS_EPS = 1e-6
QKV_COLS = 3 * DIFF_WIDTH + 3 * MOBA_WIDTH
LAM_INIT = 0.8 - 0.6 * math.exp(-0.3 * 0)

LANES = 128
SUBLANES = 8
VMEM_LIMIT_BYTES = 60 * 1024 * 1024
MASK_VALUE = -1e30

FFN_TOKEN_TILE = 1024
FFN_FF_TILE = 512
PROJ_TOKEN_TILE = 1024
PROJ_COL_TILE = 1024
MIX_TOKEN_TILE = 512
MIX_COL_TILE = 512
ATTN_TILE = 256

_NT_DIMS = (((1,), (1,)), ((), ()))


def _rms_normalize(x, gain):
    ms = jnp.mean(x * x, axis=-1, keepdims=True)
    return x * lax.rsqrt(ms + RMS_EPS) * gain


def _compiler_params(semantics):
    return pltpu.CompilerParams(dimension_semantics=semantics, vmem_limit_bytes=VMEM_LIMIT_BYTES)


def _ffn_kernel(x_hbm, g_ref, wg_ref, wu_ref, wd_ref, gf_ref, o_ref, h_ref, x_buf, x_sem,
                *, n_ff_tiles, final_norm):
    i = pl.program_id(0)
    f = pl.program_id(1)
    tm = x_buf.shape[0]
    bf16 = jnp.bfloat16

    def x_copy(tile):
        return pltpu.make_async_copy(x_hbm.at[pl.ds(tile * tm, tm), :], x_buf, x_sem)

    @pl.when((i == 0) & (f == 0))
    def _():
        x_copy(0).start()

    @pl.when(f == 0)
    def _():
        x_copy(i).wait()
        x = x_buf[...]
        h_ref[...] = _rms_normalize(x, g_ref[...]).astype(bf16)
        o_ref[...] = x

    @pl.when((f == 1) & (i + 1 < pl.num_programs(0)))
    def _():
        x_copy(i + 1).start()

    h = h_ref[...]
    gate = jnp.dot(h, wg_ref[...].astype(bf16), preferred_element_type=jnp.float32)
    up = jnp.dot(h, wu_ref[...].astype(bf16), preferred_element_type=jnp.float32)
    act = (gate * jax.nn.sigmoid(gate) * up * 0.5).astype(bf16)
    o_ref[...] += jnp.dot(act, wd_ref[...].astype(bf16), preferred_element_type=jnp.float32)

    if final_norm:
        @pl.when(f == n_ff_tiles - 1)
        def _():
            o_ref[...] = _rms_normalize(o_ref[...], gf_ref[...])


def _ffn(x, gain, w_gu, w_down, final_gain, *, final_norm):
    t, d = x.shape
    d_ff = w_down.shape[0]
    tm, tf = FFN_TOKEN_TILE, FFN_FF_TILE
    n_ff_tiles = d_ff // tf
    assert t % tm == 0 and d_ff % tf == 0 and n_ff_tiles >= 2
    kern = functools.partial(_ffn_kernel, n_ff_tiles=n_ff_tiles, final_norm=final_norm)
    return pl.pallas_call(
        kern,
        grid=(t // tm, n_ff_tiles),
        in_specs=[
            pl.BlockSpec(memory_space=pl.ANY),
            pl.BlockSpec((1, d), lambda i, f: (0, 0)),
            pl.BlockSpec((d, tf), lambda i, f: (0, f)),
            pl.BlockSpec((d, tf), lambda i, f: (0, f + n_ff_tiles)),
            pl.BlockSpec((tf, d), lambda i, f: (f, 0)),
            pl.BlockSpec((1, d), lambda i, f: (0, 0)),
        ],
        out_specs=pl.BlockSpec((tm, d), lambda i, f: (i, 0)),
        out_shape=jax.ShapeDtypeStruct((t, d), jnp.float32),
        scratch_shapes=[pltpu.VMEM((tm, d), jnp.bfloat16),
                        pltpu.VMEM((tm, d), jnp.float32),
                        pltpu.SemaphoreType.DMA(())],
        compiler_params=_compiler_params(("arbitrary", "arbitrary")),
        name="ffn_final" if final_norm else "ffn",
    )(x, gain, w_gu, w_gu, w_down, final_gain)


def _qkv_proj_kernel(x_ref, g_ref, w_ref, s_ref, o_ref, h_ref):
    @pl.when(pl.program_id(1) == 0)
    def _():
        h_ref[...] = _rms_normalize(x_ref[...], g_ref[...]).astype(jnp.bfloat16)

    acc = jnp.dot(h_ref[...], w_ref[...].astype(jnp.bfloat16), preferred_element_type=jnp.float32)
    o_ref[...] = (acc * s_ref[...]).astype(o_ref.dtype)


def _qkv_proj(x, gain, w_in, col_scale):
    t, d = x.shape
    tm, tn = PROJ_TOKEN_TILE, PROJ_COL_TILE
    assert t % tm == 0 and QKV_COLS % tn == 0
    return pl.pallas_call(
        _qkv_proj_kernel,
        grid=(t // tm, QKV_COLS // tn),
        in_specs=[
            pl.BlockSpec((tm, d), lambda i, j: (i, 0)),
            pl.BlockSpec((1, d), lambda i, j: (0, 0)),
            pl.BlockSpec((d, tn), lambda i, j: (0, j)),
            pl.BlockSpec((1, tn), lambda i, j: (0, j)),
        ],
        out_specs=pl.BlockSpec((tm, tn), lambda i, j: (i, j)),
        out_shape=jax.ShapeDtypeStruct((t, QKV_COLS), jnp.bfloat16),
        scratch_shapes=[pltpu.VMEM((tm, d), jnp.bfloat16)],
        compiler_params=_compiler_params(("parallel", "arbitrary")),
        name="qkv_proj",
    )(x, gain, w_in, col_scale)


ALIBI_TERMS = 3


def _alibi_columns(slope, t):
    lane = lax.broadcasted_iota(jnp.int32, (t, LANES), 1)
    rest = lax.broadcasted_iota(jnp.int32, (t, LANES), 0).astype(jnp.float32) * slope
    k_extra = jnp.zeros((t, LANES), jnp.float32)
    for i in range(ALIBI_TERMS):
        term = rest.astype(jnp.bfloat16).astype(jnp.float32)
        k_extra = jnp.where(lane == i, term, k_extra)
        rest = rest - term
    q_extra = jnp.where(lane < ALIBI_TERMS, 1.0, 0.0)
    return k_extra.astype(jnp.bfloat16), q_extra.astype(jnp.bfloat16)


def _query_shift(slope, t):
    return lax.broadcasted_iota(jnp.int32, (1, t), 1).astype(jnp.float32) * (-slope)


def _causal_mask_t(t):
    kk = lax.broadcasted_iota(jnp.int32, (t, t), 0)
    qq = lax.broadcasted_iota(jnp.int32, (t, t), 1)
    return jnp.where(qq >= kk, 0.0, MASK_VALUE)


def _fold_rows(x, op):
    rows, cols = x.shape
    return op(x.reshape(rows // SUBLANES, SUBLANES, cols), axis=0)


def _score_tiles(k_tile, q, shifts, mask_diag, s_ref):
    n, t = len(shifts), q.shape[0]
    cand = None
    for j in range(n):
        sj = lax.dot_general(k_tile(j), q, _NT_DIMS, preferred_element_type=jnp.float32)
        if j == n - 1:
            sj = sj + mask_diag
        s_ref[j * t:(j + 1) * t, :] = sj
        cj = _fold_rows(sj, jnp.max) + shifts[j]
        cand = cj if cand is None else jnp.maximum(cand, cj)
    return jnp.max(cand, axis=0, keepdims=True)


def _softmax_pv(s_ref, m, shifts, vt_tile):
    t = m.shape[1]
    lpart, acc = None, None
    for j in range(len(shifts)):
        sj = s_ref[j * t:(j + 1) * t, :]
        p = jnp.exp2(sj - (m - shifts[j]))
        pj = _fold_rows(p, jnp.sum)
        lpart = pj if lpart is None else lpart + pj
        vt = vt_tile(j)
        pv = jnp.dot(vt, p.astype(vt.dtype), preferred_element_type=jnp.float32)
        acc = pv if acc is None else acc + pv
    return acc / jnp.sum(lpart, axis=0, keepdims=True)


def _run_pipelined(stage_a, stage_b, n):
    pending = stage_a(0)
    for u in range(n):
        nxt = stage_a(u + 1) if u + 1 < n else None
        stage_b(u, pending)
        pending = nxt


def _diff_attn_kernel(slopes_ref, lq1_ref, lk1_ref, lq2_ref, lk2_ref, gs_ref,
                      q_ref, k_ref, v_ref, o_ref, vt_ref, s_ref):
    slope = slopes_ref[pl.program_id(1)]
    t = ATTN_TILE
    n_tiles = q_ref.shape[1] // t

    vt_ref[...] = v_ref[0].T
    k_extra, q_extra = _alibi_columns(slope, t)
    mask_diag = _causal_mask_t(t)
    q_shift = _query_shift(slope, t)
    lam = (jnp.exp(jnp.sum(lq1_ref[...] * lk1_ref[...], axis=-1, keepdims=True))
           - jnp.exp(jnp.sum(lq2_ref[...] * lk2_ref[...], axis=-1, keepdims=True))
           + LAM_INIT)
    lane = lax.broadcasted_iota(jnp.int32, (t, LANES), 1)
    k_tile = lambda j: jnp.concatenate([k_ref[0, j * t:(j + 1) * t, :], k_extra], axis=1)
    vt_tile = lambda j: vt_ref[:, j * t:(j + 1) * t]
    shifts = lambda c: [q_shift - slope * float((c - j) * t) for j in range(c + 1)]

    def scores(c):
        q = q_ref[0, c * t:(c + 1) * t, :]
        zero = jnp.zeros_like(q)
        q_maps = (jnp.where(lane < DIFF_HEAD_DIM, q, zero), jnp.where(lane >= DIFF_HEAD_DIM, q, zero))
        return [_score_tiles(k_tile, jnp.concatenate([qm, q_extra], axis=1), shifts(c), mask_diag,
                             s_ref.at[c % 2, mi])
                for mi, qm in enumerate(q_maps)]

    def finish(c, maxes):
        o0, o1 = [_softmax_pv(s_ref.at[c % 2, mi], m, shifts(c), vt_tile) for mi, m in enumerate(maxes)]
        o = (o0 - lam * o1).T
        o_ref[0, c * t:(c + 1) * t, :] = (
            _rms_normalize(o, gs_ref[...]) * (1.0 - LAM_INIT)).astype(o_ref.dtype)

    _run_pipelined(scores, finish, n_tiles)


def _diff_attn(qkv, slopes, lam_q1, lam_k1, lam_q2, lam_k2, g_subln):
    b, s, _ = qkv.shape
    nh, hw = N_HEADS_DIFF, 2 * DIFF_HEAD_DIM
    assert hw == LANES and s % ATTN_TILE == 0
    k_blk0, v_blk0 = DIFF_WIDTH // hw, 2 * DIFF_WIDTH // hw
    lam_spec = pl.BlockSpec((1, DIFF_HEAD_DIM), lambda bi, h: (0, 0))
    return pl.pallas_call(
        _diff_attn_kernel,
        grid=(b, nh),
        in_specs=[
            pl.BlockSpec(memory_space=pltpu.SMEM),
            lam_spec, lam_spec, lam_spec, lam_spec,
            pl.BlockSpec((1, hw), lambda bi, h: (0, 0)),
            pl.BlockSpec((1, s, hw), lambda bi, h: (bi, 0, h)),
            pl.BlockSpec((1, s, hw), lambda bi, h: (bi, 0, k_blk0 + h)),
            pl.BlockSpec((1, s, hw), lambda bi, h: (bi, 0, v_blk0 + h)),
        ],
        out_specs=pl.BlockSpec((1, s, hw), lambda bi, h: (bi, 0, h)),
        out_shape=jax.ShapeDtypeStruct((b, s, DIFF_WIDTH), jnp.bfloat16),
        scratch_shapes=[pltpu.VMEM((hw, s), jnp.bfloat16),
                        pltpu.VMEM((2, 2, s, ATTN_TILE), jnp.float32)],
        compiler_params=_compiler_params(("parallel", "parallel")),
        name="diff_attn",
    )(slopes, lam_q1, lam_k1, lam_q2, lam_k2, g_subln, qkv, qkv, qkv)


def _moba_attn_kernel(slopes_ref, q_ref, k_ref, v_ref, o_ref, vt_ref, s_ref, *, n_blocks, n_sel):
    slope = slopes_ref[pl.program_id(1)]
    blk, dh = MOBA_BLOCK, MOBA_HEAD_DIM

    vt_ref[...] = v_ref[0].T
    k_extra, q_extra = _alibi_columns(slope, blk)
    mask_diag = _causal_mask_t(blk)
    q_shift = _query_shift(slope, blk)

    kmean = jnp.mean(k_ref[0].astype(jnp.float32).reshape(n_blocks, blk, dh), axis=1)
    kmean = jnp.concatenate([kmean, jnp.zeros((LANES - n_blocks, dh), jnp.float32)], axis=0)
    kmean_hi = kmean.astype(jnp.bfloat16)
    kmean_lo = (kmean - kmean_hi.astype(jnp.float32)).astype(jnp.bfloat16)
    blk_id = lax.broadcasted_iota(jnp.int32, (n_blocks, blk), 0)
    k_tile = lambda j: jnp.concatenate([k_ref[0, j * blk:(j + 1) * blk, :], k_extra], axis=1)
    vt_tile = lambda j: vt_ref[:, j * blk:(j + 1) * blk]

    def scores(c):
        q = q_ref[0, c * blk:(c + 1) * blk, :]
        g = (lax.dot_general(kmean_hi, q, _NT_DIMS, preferred_element_type=jnp.float32)
             + lax.dot_general(kmean_lo, q, _NT_DIMS, preferred_element_type=jnp.float32))[:n_blocks]
        past = blk_id < c
        g = jnp.where(past, g, -jnp.inf)
        rank = jnp.zeros(g.shape, jnp.int32)
        for r in range(1, n_blocks):
            other = pltpu.roll(g, r, 0)
            rank += jnp.where(blk_id >= r, (other >= g).astype(jnp.int32), (other > g).astype(jnp.int32))
        keep = (past & (rank < n_sel)) | (blk_id == c)
        sel_bias = jnp.where(keep, 0.0, MASK_VALUE)
        shifts = [sel_bias[j:j + 1, :] + (q_shift - slope * float((c - j) * blk)) for j in range(c + 1)]
        q_aug = jnp.concatenate([q, q_extra], axis=1)
        return _score_tiles(k_tile, q_aug, shifts, mask_diag, s_ref.at[c % 2]), shifts

    def finish(c, state):
        m, shifts = state
        o = _softmax_pv(s_ref.at[c % 2], m, shifts, vt_tile)
        o_ref[0, c * blk:(c + 1) * blk, :] = o.T.astype(o_ref.dtype)

    _run_pipelined(scores, finish, n_blocks)


def _moba_attn(qkv, slopes):
    b, s, _ = qkv.shape
    nh, dh, blk = N_HEADS_MOBA, MOBA_HEAD_DIM, MOBA_BLOCK
    assert dh == LANES and s % blk == 0
    n_blocks = s // blk
    assert n_blocks == SUBLANES, "block ranking uses one vreg row per MoBA block"
    n_sel = min(MOBA_TOPK, n_blocks - 1)
    q_blk0 = 3 * DIFF_WIDTH // dh
    k_blk0, v_blk0 = q_blk0 + nh, q_blk0 + 2 * nh
    kern = functools.partial(_moba_attn_kernel, n_blocks=n_blocks, n_sel=n_sel)
    return pl.pallas_call(
        kern,
        grid=(b, nh),
        in_specs=[
            pl.BlockSpec(memory_space=pltpu.SMEM),
            pl.BlockSpec((1, s, dh), lambda bi, h: (bi, 0, q_blk0 + h)),
            pl.BlockSpec((1, s, dh), lambda bi, h: (bi, 0, k_blk0 + h)),
            pl.BlockSpec((1, s, dh), lambda bi, h: (bi, 0, v_blk0 + h)),
        ],
        out_specs=pl.BlockSpec((1, s, dh), lambda bi, h: (bi, 0, h)),
        out_shape=jax.ShapeDtypeStruct((b, s, MOBA_WIDTH), jnp.bfloat16),
        scratch_shapes=[pltpu.VMEM((dh, s), jnp.bfloat16),
                        pltpu.VMEM((2, s, blk), jnp.float32)],
        compiler_params=_compiler_params(("parallel", "parallel")),
        name="moba_attn",
    )(slopes, qkv, qkv, qkv)


def _mix_out_kernel(x_ref, g_ref, oa_ref, ob_ref, wga_ref, wgb_ref, pa_ref, pb_ref, wo_ref, o_ref, h_ref):
    @pl.when(pl.program_id(1) == 0)
    def _():
        x = x_ref[...]
        h_ref[...] = _rms_normalize(x, g_ref[...]).astype(jnp.bfloat16)
        o_ref[...] = x

    h = h_ref[...]
    gate_a = jnp.dot(h, wga_ref[...], preferred_element_type=jnp.float32)
    gate_b = jnp.dot(h, wgb_ref[...], preferred_element_type=jnp.float32)
    proj_a = jnp.dot(oa_ref[...], pa_ref[...], preferred_element_type=jnp.float32)
    proj_b = jnp.dot(ob_ref[...], pb_ref[...], preferred_element_type=jnp.float32)
    merged = jax.nn.sigmoid(gate_a) * proj_a + jax.nn.sigmoid(gate_b) * proj_b
    o_ref[...] += jnp.dot(merged.astype(jnp.bfloat16), wo_ref[...], preferred_element_type=jnp.float32)


def _mix_out(x, gain, o_a, o_b, w_gates, p_a, p_b, w_o):
    t, d = x.shape
    tm, tc = MIX_TOKEN_TILE, MIX_COL_TILE
    assert t % tm == 0 and d % tc == 0
    ga_blk0, gb_blk0 = 0, d // tc
    return pl.pallas_call(
        _mix_out_kernel,
        grid=(t // tm, d // tc),
        in_specs=[
            pl.BlockSpec((tm, d), lambda i, c: (i, 0)),
            pl.BlockSpec((1, d), lambda i, c: (0, 0)),
            pl.BlockSpec((tm, DIFF_WIDTH), lambda i, c: (i, 0)),
            pl.BlockSpec((tm, MOBA_WIDTH), lambda i, c: (i, 0)),
            pl.BlockSpec((d, tc), lambda i, c: (0, ga_blk0 + c)),
            pl.BlockSpec((d, tc), lambda i, c: (0, gb_blk0 + c)),
            pl.BlockSpec((DIFF_WIDTH, tc), lambda i, c: (0, c)),
            pl.BlockSpec((MOBA_WIDTH, tc), lambda i, c: (0, c)),
            pl.BlockSpec((tc, d), lambda i, c: (c, 0)),
        ],
        out_specs=pl.BlockSpec((tm, d), lambda i, c: (i, 0)),
        out_shape=jax.ShapeDtypeStruct((t, d), jnp.float32),
        scratch_shapes=[pltpu.VMEM((tm, d), jnp.bfloat16)],
        compiler_params=_compiler_params(("parallel", "arbitrary")),
        name="mix_out",
    )(x, gain, o_a, o_b, w_gates, w_gates, p_a, p_b, w_o)


LOG2_E = math.log2(math.e)


def _alibi_slopes(n):
    return jnp.asarray(LOG2_E * 2.0 ** (-8.0 * np.arange(1, n + 1) / n), dtype=jnp.float32)


def _qkv_col_scale():
    scale = np.ones((1, QKV_COLS), np.float32)
    scale[:, :DIFF_WIDTH] = LOG2_E * DIFF_HEAD_DIM ** -0.5
    scale[:, 3 * DIFF_WIDTH:3 * DIFF_WIDTH + MOBA_WIDTH] = LOG2_E * MOBA_HEAD_DIM ** -0.5
    return jnp.asarray(scale)


def kernel(x, g_ffn1, w_ffn1_gu, w_ffn1_down, g_mix, w_in, lam_q1, lam_k1, lam_q2, lam_k2, g_subln, p_a, p_b, w_o, g_ffn2, w_ffn2_gu, w_ffn2_down, g_final):
    b, s, d = x.shape
    assert g_ffn1.shape[0] == 1, "single-layer stack"
    bf16 = jnp.bfloat16
    xt = x.reshape(b * s, d)
    g_final_row = g_final.reshape(1, d)

    x1 = _ffn(xt, g_ffn1, w_ffn1_gu[0], w_ffn1_down[0], g_final_row, final_norm=False)

    qkv = _qkv_proj(x1, g_mix, w_in[0], _qkv_col_scale()).reshape(b, s, QKV_COLS)
    o_a = _diff_attn(qkv, _alibi_slopes(N_HEADS_DIFF), lam_q1, lam_k1, lam_q2, lam_k2, g_subln)
    o_b = _moba_attn(qkv, _alibi_slopes(N_HEADS_MOBA))
    x2 = _mix_out(x1, g_mix, o_a.reshape(b * s, DIFF_WIDTH), o_b.reshape(b * s, MOBA_WIDTH),
                  w_in[0][:, QKV_COLS:].astype(bf16), p_a[0].astype(bf16), p_b[0].astype(bf16),
                  w_o[0].astype(bf16))

    out = _ffn(x2, g_ffn2, w_ffn2_gu[0], w_ffn2_down[0], g_final_row, final_norm=True)
    return out.reshape(b, s, d)
```

```python
import functools
import math

import jax
import jax.numpy as jnp
import numpy as np
from jax import lax
from jax.experimental import pallas as pl
from jax.experimental.pallas import tpu as pltpu

D_MODEL = 2048
N_HEADS_DIFF = 8
DIFF_HEAD_DIM = 64
DIFF_WIDTH = N_HEADS_DIFF * 2 * DIFF_HEAD_DIM
N_HEADS_MOBA = 8
MOBA_HEAD_DIM = 128
MOBA_WIDTH = N_HEADS_MOBA * MOBA_HEAD_DIM
MOBA_BLOCK = 256
MOBA_TOPK = 3
D_FF = 5632
RMS_EPS = 1e-6
QKV_COLS = 3 * DIFF_WIDTH + 3 * MOBA_WIDTH
LAM_INIT = 0.8 - 0.6 * math.exp(-0.3 * 0)

LANES = 128
SUBLANES = 8
VMEM_LIMIT_BYTES = 62 * 1024 * 1024
MASK_VALUE = -1e30

FFN_TOKEN_TILE = 1024
FFN_FF_TILE = 512
PROJ_TOKEN_TILE = 2048
PROJ_COL_TILE = 1024
MIX_TOKEN_TILE = 1024
MIX_COL_TILE = 512
ATTN_TILE = 256

_NT_DIMS = (((1,), (1,)), ((), ()))


def _rms_normalize(x, gain):
    ms = jnp.mean(x * x, axis=-1, keepdims=True)
    return x * lax.rsqrt(ms + RMS_EPS) * gain


def _compiler_params(semantics):
    return pltpu.CompilerParams(dimension_semantics=semantics, vmem_limit_bytes=VMEM_LIMIT_BYTES)


def _on_token_tile(x_hbm, x_buf, x_sem, consume):
    i = pl.program_id(0)
    j = pl.program_id(1)
    tm = x_buf.shape[0]

    def copy(tile):
        return pltpu.make_async_copy(x_hbm.at[pl.ds(tile * tm, tm), :], x_buf, x_sem)

    @pl.when((i == 0) & (j == 0))
    def _():
        copy(0).start()

    @pl.when(j == 0)
    def _():
        copy(i).wait()
        consume()

    @pl.when((j == 1) & (i + 1 < pl.num_programs(0)))
    def _():
        copy(i + 1).start()


_X_IN_HBM = pl.BlockSpec(memory_space=pl.ANY)
_SEQUENTIAL_GRID = ("arbitrary", "arbitrary")


def _token_tile_scratch(tm, d):
    return [pltpu.VMEM((tm, d), jnp.float32), pltpu.SemaphoreType.DMA(())]


def _ffn_kernel(x_hbm, g_ref, wg_ref, wu_ref, wd_ref, gf_ref, o_ref, h_ref, x_buf, x_sem,
                *, n_ff_tiles, final_norm):
    f = pl.program_id(1)
    bf16 = jnp.bfloat16

    def start_tile():
        x = x_buf[...]
        h_ref[...] = _rms_normalize(x, g_ref[...]).astype(bf16)
        o_ref[...] = x

    _on_token_tile(x_hbm, x_buf, x_sem, start_tile)

    h = h_ref[...]
    gate = jnp.dot(h, wg_ref[...].astype(bf16), preferred_element_type=jnp.float32)
    up = jnp.dot(h, wu_ref[...].astype(bf16), preferred_element_type=jnp.float32)
    act = (gate * jax.nn.sigmoid(gate) * up * 0.5).astype(bf16)
    o_ref[...] += jnp.dot(act, wd_ref[...].astype(bf16), preferred_element_type=jnp.float32)

    if final_norm:
        @pl.when(f == n_ff_tiles - 1)
        def _():
            o_ref[...] = _rms_normalize(o_ref[...], gf_ref[...])


def _ffn(x, gain, w_gu, w_down, final_gain, *, final_norm):
    t, d = x.shape
    d_ff = w_down.shape[0]
    tm, tf = FFN_TOKEN_TILE, FFN_FF_TILE
    n_ff_tiles = d_ff // tf
    assert t % tm == 0 and d_ff % tf == 0 and n_ff_tiles >= 2
    kern = functools.partial(_ffn_kernel, n_ff_tiles=n_ff_tiles, final_norm=final_norm)
    return pl.pallas_call(
        kern,
        grid=(t // tm, n_ff_tiles),
        in_specs=[
            _X_IN_HBM,
            pl.BlockSpec((1, d), lambda i, f: (0, 0)),
            pl.BlockSpec((d, tf), lambda i, f: (0, f)),
            pl.BlockSpec((d, tf), lambda i, f: (0, f + n_ff_tiles)),
            pl.BlockSpec((tf, d), lambda i, f: (f, 0)),
            pl.BlockSpec((1, d), lambda i, f: (0, 0)),
        ],
        out_specs=pl.BlockSpec((tm, d), lambda i, f: (i, 0)),
        out_shape=jax.ShapeDtypeStruct((t, d), jnp.float32),
        scratch_shapes=[pltpu.VMEM((tm, d), jnp.bfloat16)] + _token_tile_scratch(tm, d),
        compiler_params=_compiler_params(_SEQUENTIAL_GRID),
        name="ffn_final" if final_norm else "ffn",
    )(x, gain, w_gu, w_gu, w_down, final_gain)


def _qkv_proj_kernel(x_hbm, g_ref, w_ref, s_ref, o_ref, h_ref, x_buf, x_sem):
    def start_tile():
        h_ref[...] = _rms_normalize(x_buf[...], g_ref[...]).astype(jnp.bfloat16)

    _on_token_tile(x_hbm, x_buf, x_sem, start_tile)

    acc = jnp.dot(h_ref[...], w_ref[...].astype(jnp.bfloat16), preferred_element_type=jnp.float32)
    o_ref[...] = (acc * s_ref[...]).astype(o_ref.dtype)


def _qkv_proj(x, gain, w_in, col_scale):
    t, d = x.shape
    tm, tn = PROJ_TOKEN_TILE, PROJ_COL_TILE
    assert t % tm == 0 and QKV_COLS % tn == 0 and QKV_COLS // tn >= 2
    return pl.pallas_call(
        _qkv_proj_kernel,
        grid=(t // tm, QKV_COLS // tn),
        in_specs=[
            _X_IN_HBM,
            pl.BlockSpec((1, d), lambda i, j: (0, 0)),
            pl.BlockSpec((d, tn), lambda i, j: (0, j)),
            pl.BlockSpec((1, tn), lambda i, j: (0, j)),
        ],
        out_specs=pl.BlockSpec((tm, tn), lambda i, j: (i, j)),
        out_shape=jax.ShapeDtypeStruct((t, QKV_COLS), jnp.bfloat16),
        scratch_shapes=[pltpu.VMEM((tm, d), jnp.bfloat16)] + _token_tile_scratch(tm, d),
        compiler_params=_compiler_params(_SEQUENTIAL_GRID),
        name="qkv_proj",
    )(x, gain, w_in, col_scale)


ALIBI_TERMS = 3


def _alibi_columns(slope, t):
    lane = lax.broadcasted_iota(jnp.int32, (t, LANES), 1)
    rest = lax.broadcasted_iota(jnp.int32, (t, LANES), 0).astype(jnp.float32) * slope
    k_extra = jnp.zeros((t, LANES), jnp.float32)
    for i in range(ALIBI_TERMS):
        term = rest.astype(jnp.bfloat16).astype(jnp.float32)
        k_extra = jnp.where(lane == i, term, k_extra)
        rest = rest - term
    q_extra = jnp.where(lane < ALIBI_TERMS, 1.0, 0.0)
    return k_extra.astype(jnp.bfloat16), q_extra.astype(jnp.bfloat16)


def _query_shift(slope, t):
    return lax.broadcasted_iota(jnp.int32, (1, t), 1).astype(jnp.float32) * (-slope)


def _causal_mask_t(t):
    kk = lax.broadcasted_iota(jnp.int32, (t, t), 0)
    qq = lax.broadcasted_iota(jnp.int32, (t, t), 1)
    return jnp.where(qq >= kk, 0.0, MASK_VALUE)


def _fold_rows(x, op):
    rows, cols = x.shape
    return op(x.reshape(rows // SUBLANES, SUBLANES, cols), axis=0)


def _score_tiles(k_tile, q, shifts, mask_diag, s_ref):
    n, t = len(shifts), q.shape[0]
    cand = None
    for j in range(n):
        sj = lax.dot_general(k_tile(j), q, _NT_DIMS, preferred_element_type=jnp.float32)
        if j == n - 1:
            sj = sj + mask_diag
        s_ref[j * t:(j + 1) * t, :] = sj
        cj = _fold_rows(sj, jnp.max) + shifts[j]
        cand = cj if cand is None else jnp.maximum(cand, cj)
    return jnp.max(cand, axis=0, keepdims=True)


def _softmax_pv(s_ref, m, shifts, vt_tile):
    t = m.shape[1]
    lpart, acc = None, None
    for j in range(len(shifts)):
        sj = s_ref[j * t:(j + 1) * t, :]
        p = jnp.exp2(sj - (m - shifts[j]))
        pj = _fold_rows(p, jnp.sum)
        lpart = pj if lpart is None else lpart + pj
        vt = vt_tile(j)
        pv = jnp.dot(vt, p.astype(vt.dtype), preferred_element_type=jnp.float32)
        acc = pv if acc is None else acc + pv
    return acc / jnp.sum(lpart, axis=0, keepdims=True)


def _run_pipelined(stage_a, stage_b, n):
    pending = stage_a(0)
    for u in range(n):
        nxt = stage_a(u + 1) if u + 1 < n else None
        stage_b(u, pending)
        pending = nxt


def _diff_attn_kernel(slopes_ref, lq1_ref, lk1_ref, lq2_ref, lk2_ref, gs_ref,
                      q_ref, k_ref, v_ref, o_ref, vt_ref, s_ref):
    slope = slopes_ref[pl.program_id(1)]
    t = ATTN_TILE
    n_tiles = q_ref.shape[1] // t

    vt_ref[...] = v_ref[0].T
    k_extra, q_extra = _alibi_columns(slope, t)
    mask_diag = _causal_mask_t(t)
    q_shift = _query_shift(slope, t)
    lam = (jnp.exp(jnp.sum(lq1_ref[...] * lk1_ref[...], axis=-1, keepdims=True))
           - jnp.exp(jnp.sum(lq2_ref[...] * lk2_ref[...], axis=-1, keepdims=True))
           + LAM_INIT)
    lane = lax.broadcasted_iota(jnp.int32, (t, LANES), 1)
    k_tile = lambda j: jnp.concatenate([k_ref[0, j * t:(j + 1) * t, :], k_extra], axis=1)
    vt_tile = lambda j: vt_ref[:, j * t:(j + 1) * t]
    shifts = lambda c: [q_shift - slope * float((c - j) * t) for j in range(c + 1)]

    def scores(c):
        q = q_ref[0, c * t:(c + 1) * t, :]
        zero = jnp.zeros_like(q)
        q_maps = (jnp.where(lane < DIFF_HEAD_DIM, q, zero), jnp.where(lane >= DIFF_HEAD_DIM, q, zero))
        return [_score_tiles(k_tile, jnp.concatenate([qm, q_extra], axis=1), shifts(c), mask_diag,
                             s_ref.at[c % 2, mi])
                for mi, qm in enumerate(q_maps)]

    def finish(c, maxes):
        o0, o1 = [_softmax_pv(s_ref.at[c % 2, mi], m, shifts(c), vt_tile) for mi, m in enumerate(maxes)]
        o = (o0 - lam * o1).T
        o_ref[0, c * t:(c + 1) * t, :] = (
            _rms_normalize(o, gs_ref[...]) * (1.0 - LAM_INIT)).astype(o_ref.dtype)

    _run_pipelined(scores, finish, n_tiles)


def _diff_attn(qkv, slopes, lam_q1, lam_k1, lam_q2, lam_k2, g_subln):
    b, s, _ = qkv.shape
    nh, hw = N_HEADS_DIFF, 2 * DIFF_HEAD_DIM
    assert hw == LANES and s % ATTN_TILE == 0
    k_blk0, v_blk0 = DIFF_WIDTH // hw, 2 * DIFF_WIDTH // hw
    lam_spec = pl.BlockSpec((1, DIFF_HEAD_DIM), lambda bi, h: (0, 0))
    return pl.pallas_call(
        _diff_attn_kernel,
        grid=(b, nh),
        in_specs=[
            pl.BlockSpec(memory_space=pltpu.SMEM),
            lam_spec, lam_spec, lam_spec, lam_spec,
            pl.BlockSpec((1, hw), lambda bi, h: (0, 0)),
            pl.BlockSpec((1, s, hw), lambda bi, h: (bi, 0, h)),
            pl.BlockSpec((1, s, hw), lambda bi, h: (bi, 0, k_blk0 + h)),
            pl.BlockSpec((1, s, hw), lambda bi, h: (bi, 0, v_blk0 + h)),
        ],
        out_specs=pl.BlockSpec((1, s, hw), lambda bi, h: (bi, 0, h)),
        out_shape=jax.ShapeDtypeStruct((b, s, DIFF_WIDTH), jnp.bfloat16),
        scratch_shapes=[pltpu.VMEM((hw, s), jnp.bfloat16),
                        pltpu.VMEM((2, 2, s, ATTN_TILE), jnp.float32)],
        compiler_params=_compiler_params(("parallel", "parallel")),
        name="diff_attn",
    )(slopes, lam_q1, lam_k1, lam_q2, lam_k2, g_subln, qkv, qkv, qkv)


def _moba_attn_kernel(slopes_ref, q_ref, k_ref, v_ref, o_ref, vt_ref, s_ref, *, n_blocks, n_sel):
    slope = slopes_ref[pl.program_id(1)]
    blk, dh = MOBA_BLOCK, MOBA_HEAD_DIM

    vt_ref[...] = v_ref[0].T
    k_extra, q_extra = _alibi_columns(slope, blk)
    mask_diag = _causal_mask_t(blk)
    q_shift = _query_shift(slope, blk)

    kmean = jnp.mean(k_ref[0].astype(jnp.float32).reshape(n_blocks, blk, dh), axis=1)
    kmean = jnp.concatenate([kmean, jnp.zeros((LANES - n_blocks, dh), jnp.float32)], axis=0)
    kmean_hi = kmean.astype(jnp.bfloat16)
    kmean_lo = (kmean - kmean_hi.astype(jnp.float32)).astype(jnp.bfloat16)
    blk_id = lax.broadcasted_iota(jnp.int32, (n_blocks, blk), 0)
    k_tile = lambda j: jnp.concatenate([k_ref[0, j * blk:(j + 1) * blk, :], k_extra], axis=1)
    vt_tile = lambda j: vt_ref[:, j * blk:(j + 1) * blk]

    def scores(c):
        q = q_ref[0, c * blk:(c + 1) * blk, :]
        g = (lax.dot_general(kmean_hi, q, _NT_DIMS, preferred_element_type=jnp.float32)
             + lax.dot_general(kmean_lo, q, _NT_DIMS, preferred_element_type=jnp.float32))[:n_blocks]
        past = blk_id < c
        g = jnp.where(past, g, -jnp.inf)
        rank = jnp.zeros(g.shape, jnp.int32)
        for r in range(1, n_blocks):
            other = pltpu.roll(g, r, 0)
            rank += jnp.where(blk_id >= r, (other >= g).astype(jnp.int32), (other > g).astype(jnp.int32))
        keep = (past & (rank < n_sel)) | (blk_id == c)
        sel_bias = jnp.where(keep, 0.0, MASK_VALUE)
        shifts = [sel_bias[j:j + 1, :] + (q_shift - slope * float((c - j) * blk)) for j in range(c + 1)]
        q_aug = jnp.concatenate([q, q_extra], axis=1)
        return _score_tiles(k_tile, q_aug, shifts, mask_diag, s_ref.at[c % 2]), shifts

    def finish(c, state):
        m, shifts = state
        o = _softmax_pv(s_ref.at[c % 2], m, shifts, vt_tile)
        o_ref[0, c * blk:(c + 1) * blk, :] = o.T.astype(o_ref.dtype)

    _run_pipelined(scores, finish, n_blocks)


def _moba_attn(qkv, slopes):
    b, s, _ = qkv.shape
    nh, dh, blk = N_HEADS_MOBA, MOBA_HEAD_DIM, MOBA_BLOCK
    assert dh == LANES and s % blk == 0
    n_blocks = s // blk
    assert n_blocks == SUBLANES, "block ranking uses one vreg row per MoBA block"
    n_sel = min(MOBA_TOPK, n_blocks - 1)
    q_blk0 = 3 * DIFF_WIDTH // dh
    k_blk0, v_blk0 = q_blk0 + nh, q_blk0 + 2 * nh
    kern = functools.partial(_moba_attn_kernel, n_blocks=n_blocks, n_sel=n_sel)
    return pl.pallas_call(
        kern,
        grid=(b, nh),
        in_specs=[
            pl.BlockSpec(memory_space=pltpu.SMEM),
            pl.BlockSpec((1, s, dh), lambda bi, h: (bi, 0, q_blk0 + h)),
            pl.BlockSpec((1, s, dh), lambda bi, h: (bi, 0, k_blk0 + h)),
            pl.BlockSpec((1, s, dh), lambda bi, h: (bi, 0, v_blk0 + h)),
        ],
        out_specs=pl.BlockSpec((1, s, dh), lambda bi, h: (bi, 0, h)),
        out_shape=jax.ShapeDtypeStruct((b, s, MOBA_WIDTH), jnp.bfloat16),
        scratch_shapes=[pltpu.VMEM((dh, s), jnp.bfloat16),
                        pltpu.VMEM((2, s, blk), jnp.float32)],
        compiler_params=_compiler_params(("parallel", "parallel")),
        name="moba_attn",
    )(slopes, qkv, qkv, qkv)


def _mix_out_kernel(x_hbm, g_ref, oa_ref, ob_ref, wga_ref, wgb_ref, pa_ref, pb_ref, wo_ref, o_ref,
                    h_ref, x_buf, x_sem):
    def start_tile():
        x = x_buf[...]
        h_ref[...] = _rms_normalize(x, g_ref[...]).astype(jnp.bfloat16)
        o_ref[...] = x

    _on_token_tile(x_hbm, x_buf, x_sem, start_tile)

    h = h_ref[...]
    gate_a = jnp.dot(h, wga_ref[...], preferred_element_type=jnp.float32)
    gate_b = jnp.dot(h, wgb_ref[...], preferred_element_type=jnp.float32)
    proj_a = jnp.dot(oa_ref[...], pa_ref[...], preferred_element_type=jnp.float32)
    proj_b = jnp.dot(ob_ref[...], pb_ref[...], preferred_element_type=jnp.float32)
    merged = jax.nn.sigmoid(gate_a) * proj_a + jax.nn.sigmoid(gate_b) * proj_b
    o_ref[...] += jnp.dot(merged.astype(jnp.bfloat16), wo_ref[...], preferred_element_type=jnp.float32)


def _mix_out(x, gain, o_a, o_b, w_gates, p_a, p_b, w_o):
    t, d = x.shape
    tm, tc = MIX_TOKEN_TILE, MIX_COL_TILE
    assert t % tm == 0 and d % tc == 0 and d // tc >= 2
    ga_blk0, gb_blk0 = 0, d // tc
    return pl.pallas_call(
        _mix_out_kernel,
        grid=(t // tm, d // tc),
        in_specs=[
            _X_IN_HBM,
            pl.BlockSpec((1, d), lambda i, c: (0, 0)),
            pl.BlockSpec((tm, DIFF_WIDTH), lambda i, c: (i, 0)),
            pl.BlockSpec((tm, MOBA_WIDTH), lambda i, c: (i, 0)),
            pl.BlockSpec((d, tc), lambda i, c: (0, ga_blk0 + c)),
            pl.BlockSpec((d, tc), lambda i, c: (0, gb_blk0 + c)),
            pl.BlockSpec((DIFF_WIDTH, tc), lambda i, c: (0, c)),
            pl.BlockSpec((MOBA_WIDTH, tc), lambda i, c: (0, c)),
            pl.BlockSpec((tc, d), lambda i, c: (c, 0)),
        ],
        out_specs=pl.BlockSpec((tm, d), lambda i, c: (i, 0)),
        out_shape=jax.ShapeDtypeStruct((t, d), jnp.float32),
        scratch_shapes=[pltpu.VMEM((tm, d), jnp.bfloat16)] + _token_tile_scratch(tm, d),
        compiler_params=_compiler_params(_SEQUENTIAL_GRID),
        name="mix_out",
    )(x, gain, o_a, o_b, w_gates, w_gates, p_a, p_b, w_o)


LOG2_E = math.log2(math.e)


def _alibi_slopes(n):
    return jnp.asarray(LOG2_E * 2.0 ** (-8.0 * np.arange(1, n + 1) / n), dtype=jnp.float32)


def _qkv_col_scale():
    scale = np.ones((1, QKV_COLS), np.float32)
    scale[:, :DIFF_WIDTH] = LOG2_E * DIFF_HEAD_DIM ** -0.5
    scale[:, 3 * DIFF_WIDTH:3 * DIFF_WIDTH + MOBA_WIDTH] = LOG2_E * MOBA_HEAD_DIM ** -0.5
    return jnp.asarray(scale)


def kernel(x, g_ffn1, w_ffn1_gu, w_ffn1_down, g_mix, w_in, lam_q1, lam_k1, lam_q2, lam_k2, g_subln, p_a, p_b, w_o, g_ffn2, w_ffn2_gu, w_ffn2_down, g_final):
    b, s, d = x.shape
    assert g_ffn1.shape[0] == 1, "single-layer stack"
    bf16 = jnp.bfloat16
    xt = x.reshape(b * s, d)
    g_final_row = g_final.reshape(1, d)

    x1 = _ffn(xt, g_ffn1, w_ffn1_gu[0], w_ffn1_down[0], g_final_row, final_norm=False)

    qkv = _qkv_proj(x1, g_mix, w_in[0], _qkv_col_scale()).reshape(b, s, QKV_COLS)
    o_a = _diff_attn(qkv, _alibi_slopes(N_HEADS_DIFF), lam_q1, lam_k1, lam_q2, lam_k2, g_subln)
    o_b = _moba_attn(qkv, _alibi_slopes(N_HEADS_MOBA))
    x2 = _mix_out(x1, g_mix, o_a.reshape(b * s, DIFF_WIDTH), o_b.reshape(b * s, MOBA_WIDTH),
                  w_in[0][:, QKV_COLS:].astype(bf16), p_a[0].astype(bf16), p_b[0].astype(bf16),
                  w_o[0].astype(bf16))

    out = _ffn(x2, g_ffn2, w_ffn2_gu[0], w_ffn2_down[0], g_final_row, final_norm=True)
    return out.reshape(b, s, d)
```

```python
import functools
import itertools
import math

import jax
import jax.numpy as jnp
import numpy as np
from jax import lax
from jax.experimental import pallas as pl
from jax.experimental.pallas import tpu as pltpu

D_MODEL = 2048
N_HEADS_DIFF = 8
DIFF_HEAD_DIM = 64
DIFF_WIDTH = N_HEADS_DIFF * 2 * DIFF_HEAD_DIM
N_HEADS_MOBA = 8
MOBA_HEAD_DIM = 128
MOBA_WIDTH = N_HEADS_MOBA * MOBA_HEAD_DIM
MOBA_BLOCK = 256
MOBA_TOPK = 3
D_FF = 5632
RMS_EPS = 1e-6
QKV_COLS = 3 * DIFF_WIDTH + 3 * MOBA_WIDTH
LAM_INIT = 0.8 - 0.6 * math.exp(-0.3 * 0)

LANES = 128
SUBLANES = 8
VMEM_LIMIT_BYTES = 62 * 1024 * 1024
MASK_VALUE = -1e30

FFN_TOKEN_TILE = 1024
FFN_FF_TILE = 512
PROJ_TOKEN_TILE = 2048
PROJ_COL_TILE = 1024
MIX_TOKEN_TILE = 1024
MIX_COL_TILE = 512
ATTN_TILE = 256

_NT_DIMS = (((1,), (1,)), ((), ()))


def _rms_normalize(x, gain):
    ms = jnp.mean(x * x, axis=-1, keepdims=True)
    return x * lax.rsqrt(ms + RMS_EPS) * gain


def _compiler_params(semantics):
    return pltpu.CompilerParams(dimension_semantics=semantics, vmem_limit_bytes=VMEM_LIMIT_BYTES)


def _on_token_tile(x_hbm, x_buf, x_sem, consume):
    i = pl.program_id(0)
    j = pl.program_id(1)
    tm = x_buf.shape[0]

    def copy(tile):
        return pltpu.make_async_copy(x_hbm.at[pl.ds(tile * tm, tm), :], x_buf, x_sem)

    @pl.when((i == 0) & (j == 0))
    def _():
        copy(0).start()

    @pl.when(j == 0)
    def _():
        copy(i).wait()
        consume()

    @pl.when((j == 1) & (i + 1 < pl.num_programs(0)))
    def _():
        copy(i + 1).start()


_X_IN_HBM = pl.BlockSpec(memory_space=pl.ANY)
_SEQUENTIAL_GRID = ("arbitrary", "arbitrary")


def _token_tile_scratch(tm, d):
    return [pltpu.VMEM((tm, d), jnp.float32), pltpu.SemaphoreType.DMA(())]


def _ffn_kernel(x_hbm, g_ref, wg_ref, wu_ref, wd_ref, gf_ref, o_ref, h_ref, x_buf, x_sem,
                *, n_ff_tiles, final_norm):
    f = pl.program_id(1)
    bf16 = jnp.bfloat16

    def start_tile():
        x = x_buf[...]
        h_ref[...] = _rms_normalize(x, g_ref[...]).astype(bf16)
        o_ref[...] = x

    _on_token_tile(x_hbm, x_buf, x_sem, start_tile)

    h = h_ref[...]
    gate = jnp.dot(h, wg_ref[...].astype(bf16), preferred_element_type=jnp.float32)
    up = jnp.dot(h, wu_ref[...].astype(bf16), preferred_element_type=jnp.float32)
    act = (gate * jax.nn.sigmoid(gate) * up * 0.5).astype(bf16)
    o_ref[...] += jnp.dot(act, wd_ref[...].astype(bf16), preferred_element_type=jnp.float32)

    if final_norm:
        @pl.when(f == n_ff_tiles - 1)
        def _():
            o_ref[...] = _rms_normalize(o_ref[...], gf_ref[...])


def _ffn(x, gain, w_gu, w_down, final_gain, *, final_norm):
    t, d = x.shape
    d_ff = w_down.shape[0]
    tm, tf = FFN_TOKEN_TILE, FFN_FF_TILE
    n_ff_tiles = d_ff // tf
    assert t % tm == 0 and d_ff % tf == 0 and n_ff_tiles >= 2
    kern = functools.partial(_ffn_kernel, n_ff_tiles=n_ff_tiles, final_norm=final_norm)
    return pl.pallas_call(
        kern,
        grid=(t // tm, n_ff_tiles),
        in_specs=[
            _X_IN_HBM,
            pl.BlockSpec((1, d), lambda i, f: (0, 0)),
            pl.BlockSpec((d, tf), lambda i, f: (0, f)),
            pl.BlockSpec((d, tf), lambda i, f: (0, f + n_ff_tiles)),
            pl.BlockSpec((tf, d), lambda i, f: (f, 0)),
            pl.BlockSpec((1, d), lambda i, f: (0, 0)),
        ],
        out_specs=pl.BlockSpec((tm, d), lambda i, f: (i, 0)),
        out_shape=jax.ShapeDtypeStruct((t, d), jnp.float32),
        scratch_shapes=[pltpu.VMEM((tm, d), jnp.bfloat16)] + _token_tile_scratch(tm, d),
        compiler_params=_compiler_params(_SEQUENTIAL_GRID),
        name="ffn_final" if final_norm else "ffn",
    )(x, gain, w_gu, w_gu, w_down, final_gain)


def _qkv_proj_kernel(x_hbm, g_ref, w_ref, s_ref, o_ref, h_ref, x_buf, x_sem):
    def start_tile():
        h_ref[...] = _rms_normalize(x_buf[...], g_ref[...]).astype(jnp.bfloat16)

    _on_token_tile(x_hbm, x_buf, x_sem, start_tile)

    acc = jnp.dot(h_ref[...], w_ref[...].astype(jnp.bfloat16), preferred_element_type=jnp.float32)
    o_ref[...] = (acc * s_ref[...]).astype(o_ref.dtype)


def _qkv_proj(x, gain, w_in, col_scale):
    t, d = x.shape
    tm, tn = PROJ_TOKEN_TILE, PROJ_COL_TILE
    assert t % tm == 0 and QKV_COLS % tn == 0 and QKV_COLS // tn >= 2
    return pl.pallas_call(
        _qkv_proj_kernel,
        grid=(t // tm, QKV_COLS // tn),
        in_specs=[
            _X_IN_HBM,
            pl.BlockSpec((1, d), lambda i, j: (0, 0)),
            pl.BlockSpec((d, tn), lambda i, j: (0, j)),
            pl.BlockSpec((1, tn), lambda i, j: (0, j)),
        ],
        out_specs=pl.BlockSpec((tm, tn), lambda i, j: (i, j)),
        out_shape=jax.ShapeDtypeStruct((t, QKV_COLS), jnp.bfloat16),
        scratch_shapes=[pltpu.VMEM((tm, d), jnp.bfloat16)] + _token_tile_scratch(tm, d),
        compiler_params=_compiler_params(_SEQUENTIAL_GRID),
        name="qkv_proj",
    )(x, gain, w_in, col_scale)


ALIBI_TERMS = 3


def _alibi_columns(slope, t):
    lane = lax.broadcasted_iota(jnp.int32, (t, LANES), 1)
    rest = lax.broadcasted_iota(jnp.int32, (t, LANES), 0).astype(jnp.float32) * slope
    k_extra = jnp.zeros((t, LANES), jnp.float32)
    for i in range(ALIBI_TERMS):
        term = rest.astype(jnp.bfloat16).astype(jnp.float32)
        k_extra = jnp.where(lane == i, term, k_extra)
        rest = rest - term
    q_extra = jnp.where(lane < ALIBI_TERMS, 1.0, 0.0)
    return k_extra.astype(jnp.bfloat16), q_extra.astype(jnp.bfloat16)


def _query_shift(slope, t):
    return lax.broadcasted_iota(jnp.int32, (1, t), 1).astype(jnp.float32) * (-slope)


def _causal_mask_t(t):
    kk = lax.broadcasted_iota(jnp.int32, (t, t), 0)
    qq = lax.broadcasted_iota(jnp.int32, (t, t), 1)
    return jnp.where(qq >= kk, 0.0, MASK_VALUE)


def _fold_rows(x, op):
    rows, cols = x.shape
    return op(x.reshape(rows // SUBLANES, SUBLANES, cols), axis=0)


def _score_tiles(k_tile, q, shifts, mask_diag, s_ref, result):
    n, t = len(shifts), q.shape[0]
    cand = None
    for j in range(n):
        sj = lax.dot_general(k_tile(j), q, _NT_DIMS, preferred_element_type=jnp.float32)
        if j == n - 1:
            sj = sj + mask_diag
        s_ref[j * t:(j + 1) * t, :] = sj
        cj = _fold_rows(sj, jnp.max) + shifts[j]
        cand = cj if cand is None else jnp.maximum(cand, cj)
        yield
    result.append(jnp.max(cand, axis=0, keepdims=True))


def _softmax_pv(s_ref, m, shifts, vt_tile, result):
    t = m.shape[1]
    lpart, acc = None, None
    for j in range(len(shifts)):
        sj = s_ref[j * t:(j + 1) * t, :]
        p = jnp.exp2(sj - (m - shifts[j]))
        pj = _fold_rows(p, jnp.sum)
        lpart = pj if lpart is None else lpart + pj
        vt = vt_tile(j)
        pv = jnp.dot(vt, p.astype(vt.dtype), preferred_element_type=jnp.float32)
        acc = pv if acc is None else acc + pv
        yield
    result.append(acc / jnp.sum(lpart, axis=0, keepdims=True))


def _alternate(*steps):
    for _ in itertools.zip_longest(*steps):
        yield


SCORE_SLOTS = 3


def _run_pipelined(stage_a, stage_b, n):
    ahead = SCORE_SLOTS - 1
    states = {}

    def run_a(u):
        if u < n:
            steps, states[u] = stage_a(u)
            for _ in steps:
                pass

    for u in range(ahead):
        run_a(u)
    for u in range(n):
        run_a(u + ahead)
        for _ in stage_b(u, states.pop(u)):
            pass


ATTN_HEADS_PER_STEP = 2


def _diff_attn_kernel(slopes_ref, lq1_ref, lk1_ref, lq2_ref, lk2_ref, gs_ref,
                      q_ref, k_ref, v_ref, o_ref, vt_ref, s_ref):
    t, hw, nh = ATTN_TILE, 2 * DIFF_HEAD_DIM, ATTN_HEADS_PER_STEP
    n_tiles = q_ref.shape[1] // t
    mask_diag = _causal_mask_t(t)
    lam = (jnp.exp(jnp.sum(lq1_ref[...] * lk1_ref[...], axis=-1, keepdims=True))
           - jnp.exp(jnp.sum(lq2_ref[...] * lk2_ref[...], axis=-1, keepdims=True))
           + LAM_INIT)
    lane = lax.broadcasted_iota(jnp.int32, (t, LANES), 1)

    def head_setup(hh):
        cols = slice(hh * hw, (hh + 1) * hw)
        slope = slopes_ref[pl.program_id(1) * nh + hh]
        vt_ref[hh] = v_ref[0, :, cols].T
        k_extra, q_extra = _alibi_columns(slope, t)
        q_shift = _query_shift(slope, t)
        return dict(
            cols=cols, q_extra=q_extra,
            shifts=lambda c: [q_shift - slope * float((c - j) * t) for j in range(c + 1)],
            k_tile=lambda j: jnp.concatenate([k_ref[0, j * t:(j + 1) * t, cols], k_extra], axis=1),
            vt_tile=lambda j: vt_ref[hh, :, j * t:(j + 1) * t])

    heads = [head_setup(hh) for hh in range(nh)]
    streams = [(hh, mi) for hh in range(nh) for mi in range(2)]

    def scores(c):
        maxes, steps = [], []
        for hh, mi in streams:
            hd = heads[hh]
            q = q_ref[0, c * t:(c + 1) * t, hd["cols"]]
            keep = (lane < DIFF_HEAD_DIM) if mi == 0 else (lane >= DIFF_HEAD_DIM)
            qm = jnp.where(keep, q, jnp.zeros_like(q))
            maxes.append([])
            steps.append(_score_tiles(hd["k_tile"], jnp.concatenate([qm, hd["q_extra"]], axis=1),
                                      hd["shifts"](c), mask_diag,
                                      s_ref.at[c % SCORE_SLOTS, 2 * hh + mi], maxes[-1]))
        return _alternate(*steps), maxes

    def finish(c, maxes):
        outs = [[] for _ in streams]
        yield from _alternate(*[
            _softmax_pv(s_ref.at[c % SCORE_SLOTS, 2 * hh + mi], maxes[si][0], heads[hh]["shifts"](c),
                        heads[hh]["vt_tile"], outs[si])
            for si, (hh, mi) in enumerate(streams)])
        for hh, hd in enumerate(heads):
            o = (outs[2 * hh][0] - lam * outs[2 * hh + 1][0]).T
            o_ref[0, c * t:(c + 1) * t, hd["cols"]] = (
                _rms_normalize(o, gs_ref[...]) * (1.0 - LAM_INIT)).astype(o_ref.dtype)

    _run_pipelined(scores, finish, n_tiles)


def _diff_attn(qkv, slopes, lam_q1, lam_k1, lam_q2, lam_k2, g_subln):
    b, s, _ = qkv.shape
    nh, hw = N_HEADS_DIFF, 2 * DIFF_HEAD_DIM
    hps = ATTN_HEADS_PER_STEP
    w = hps * hw
    assert hw == LANES and s % ATTN_TILE == 0 and nh % hps == 0
    k_blk0, v_blk0 = DIFF_WIDTH // w, 2 * DIFF_WIDTH // w
    lam_spec = pl.BlockSpec((1, DIFF_HEAD_DIM), lambda bi, g: (0, 0))
    return pl.pallas_call(
        _diff_attn_kernel,
        grid=(b, nh // hps),
        in_specs=[
            pl.BlockSpec(memory_space=pltpu.SMEM),
            lam_spec, lam_spec, lam_spec, lam_spec,
            pl.BlockSpec((1, hw), lambda bi, g: (0, 0)),
            pl.BlockSpec((1, s, w), lambda bi, g: (bi, 0, g)),
            pl.BlockSpec((1, s, w), lambda bi, g: (bi, 0, k_blk0 + g)),
            pl.BlockSpec((1, s, w), lambda bi, g: (bi, 0, v_blk0 + g)),
        ],
        out_specs=pl.BlockSpec((1, s, w), lambda bi, g: (bi, 0, g)),
        out_shape=jax.ShapeDtypeStruct((b, s, DIFF_WIDTH), jnp.bfloat16),
        scratch_shapes=[pltpu.VMEM((hps, hw, s), jnp.bfloat16),
                        pltpu.VMEM((SCORE_SLOTS, 2 * hps, s, ATTN_TILE), jnp.float32)],
        compiler_params=_compiler_params(("parallel", "parallel")),
        name="diff_attn",
    )(slopes, lam_q1, lam_k1, lam_q2, lam_k2, g_subln, qkv, qkv, qkv)


MOBA_HEADS_PER_STEP = 2


def _moba_attn_kernel(slopes_ref, q_ref, k_ref, v_ref, o_ref, vt_ref, s_ref, *, n_blocks, n_sel):
    blk, dh, nh = MOBA_BLOCK, MOBA_HEAD_DIM, MOBA_HEADS_PER_STEP
    mask_diag = _causal_mask_t(blk)
    blk_id = lax.broadcasted_iota(jnp.int32, (n_blocks, blk), 0)

    def head_setup(hh):
        cols = slice(hh * dh, (hh + 1) * dh)
        slope = slopes_ref[pl.program_id(1) * nh + hh]
        vt_ref[hh] = v_ref[0, :, cols].T
        k_extra, q_extra = _alibi_columns(slope, blk)
        kmean = jnp.mean(k_ref[0, :, cols].astype(jnp.float32).reshape(n_blocks, blk, dh), axis=1)
        kmean = jnp.concatenate([kmean, jnp.zeros((LANES - n_blocks, dh), jnp.float32)], axis=0)
        kmean_hi = kmean.astype(jnp.bfloat16)
        return dict(
            cols=cols, slope=slope, q_extra=q_extra, q_shift=_query_shift(slope, blk),
            kmean_hi=kmean_hi, kmean_lo=(kmean - kmean_hi.astype(jnp.float32)).astype(jnp.bfloat16),
            k_tile=lambda j: jnp.concatenate([k_ref[0, j * blk:(j + 1) * blk, cols], k_extra], axis=1),
            vt_tile=lambda j: vt_ref[hh, :, j * blk:(j + 1) * blk])

    heads = [head_setup(hh) for hh in range(nh)]

    def head_scores(c, hh, hd):
        q = q_ref[0, c * blk:(c + 1) * blk, hd["cols"]]
        g = (lax.dot_general(hd["kmean_hi"], q, _NT_DIMS, preferred_element_type=jnp.float32)
             + lax.dot_general(hd["kmean_lo"], q, _NT_DIMS, preferred_element_type=jnp.float32))[:n_blocks]
        past = blk_id < c
        g = jnp.where(past, g, -jnp.inf)
        rank = jnp.zeros(g.shape, jnp.int32)
        for r in range(1, n_blocks):
            other = pltpu.roll(g, r, 0)
            rank += jnp.where(blk_id >= r, (other >= g).astype(jnp.int32), (other > g).astype(jnp.int32))
        keep = (past & (rank < n_sel)) | (blk_id == c)
        sel_bias = jnp.where(keep, 0.0, MASK_VALUE)
        shifts = [sel_bias[j:j + 1, :] + (hd["q_shift"] - hd["slope"] * float((c - j) * blk))
                  for j in range(c + 1)]
        q_aug = jnp.concatenate([q, hd["q_extra"]], axis=1)
        m = []
        steps = _score_tiles(hd["k_tile"], q_aug, shifts, mask_diag, s_ref.at[c % SCORE_SLOTS, hh], m)
        return steps, (m, shifts)

    def scores(c):
        per_head = [head_scores(c, hh, hd) for hh, hd in enumerate(heads)]
        return _alternate(*[steps for steps, _ in per_head]), [state for _, state in per_head]

    def finish(c, states):
        outs = [[] for _ in heads]
        yield from _alternate(*[
            _softmax_pv(s_ref.at[c % SCORE_SLOTS, hh], m[0], shifts, heads[hh]["vt_tile"], outs[hh])
            for hh, (m, shifts) in enumerate(states)])
        for hh, hd in enumerate(heads):
            o_ref[0, c * blk:(c + 1) * blk, hd["cols"]] = outs[hh][0].T.astype(o_ref.dtype)

    _run_pipelined(scores, finish, n_blocks)


def _moba_attn(qkv, slopes):
    b, s, _ = qkv.shape
    nh, dh, blk = N_HEADS_MOBA, MOBA_HEAD_DIM, MOBA_BLOCK
    assert dh == LANES and s % blk == 0
    n_blocks = s // blk
    assert n_blocks == SUBLANES, "block ranking uses one vreg row per MoBA block"
    n_sel = min(MOBA_TOPK, n_blocks - 1)
    hps = MOBA_HEADS_PER_STEP
    w = hps * dh
    assert nh % hps == 0 and (3 * DIFF_WIDTH) % w == 0 and MOBA_WIDTH % w == 0
    q_blk0 = 3 * DIFF_WIDTH // w
    k_blk0, v_blk0 = q_blk0 + nh // hps, q_blk0 + 2 * nh // hps
    kern = functools.partial(_moba_attn_kernel, n_blocks=n_blocks, n_sel=n_sel)
    return pl.pallas_call(
        kern,
        grid=(b, nh // hps),
        in_specs=[
            pl.BlockSpec(memory_space=pltpu.SMEM),
            pl.BlockSpec((1, s, w), lambda bi, g: (bi, 0, q_blk0 + g)),
            pl.BlockSpec((1, s, w), lambda bi, g: (bi, 0, k_blk0 + g)),
            pl.BlockSpec((1, s, w), lambda bi, g: (bi, 0, v_blk0 + g)),
        ],
        out_specs=pl.BlockSpec((1, s, w), lambda bi, g: (bi, 0, g)),
        out_shape=jax.ShapeDtypeStruct((b, s, MOBA_WIDTH), jnp.bfloat16),
        scratch_shapes=[pltpu.VMEM((hps, dh, s), jnp.bfloat16),
                        pltpu.VMEM((SCORE_SLOTS, hps, s, blk), jnp.float32)],
        compiler_params=_compiler_params(("parallel", "parallel")),
        name="moba_attn",
    )(slopes, qkv, qkv, qkv)


def _mix_out_kernel(x_hbm, g_ref, oa_ref, ob_ref, wga_ref, wgb_ref, pa_ref, pb_ref, wo_ref, o_ref,
                    h_ref, x_buf, x_sem):
    def start_tile():
        x = x_buf[...]
        h_ref[...] = _rms_normalize(x, g_ref[...]).astype(jnp.bfloat16)
        o_ref[...] = x

    _on_token_tile(x_hbm, x_buf, x_sem, start_tile)

    h = h_ref[...]
    gate_a = jnp.dot(h, wga_ref[...], preferred_element_type=jnp.float32)
    gate_b = jnp.dot(h, wgb_ref[...], preferred_element_type=jnp.float32)
    proj_a = jnp.dot(oa_ref[...], pa_ref[...], preferred_element_type=jnp.float32)
    proj_b = jnp.dot(ob_ref[...], pb_ref[...], preferred_element_type=jnp.float32)
    merged = jax.nn.sigmoid(gate_a) * proj_a + jax.nn.sigmoid(gate_b) * proj_b
    o_ref[...] += jnp.dot(merged.astype(jnp.bfloat16), wo_ref[...], preferred_element_type=jnp.float32)


def _mix_out(x, gain, o_a, o_b, w_gates, p_a, p_b, w_o):
    t, d = x.shape
    tm, tc = MIX_TOKEN_TILE, MIX_COL_TILE
    assert t % tm == 0 and d % tc == 0 and d // tc >= 2
    ga_blk0, gb_blk0 = 0, d // tc
    return pl.pallas_call(
        _mix_out_kernel,
        grid=(t // tm, d // tc),
        in_specs=[
            _X_IN_HBM,
            pl.BlockSpec((1, d), lambda i, c: (0, 0)),
            pl.BlockSpec((tm, DIFF_WIDTH), lambda i, c: (i, 0)),
            pl.BlockSpec((tm, MOBA_WIDTH), lambda i, c: (i, 0)),
            pl.BlockSpec((d, tc), lambda i, c: (0, ga_blk0 + c)),
            pl.BlockSpec((d, tc), lambda i, c: (0, gb_blk0 + c)),
            pl.BlockSpec((DIFF_WIDTH, tc), lambda i, c: (0, c)),
            pl.BlockSpec((MOBA_WIDTH, tc), lambda i, c: (0, c)),
            pl.BlockSpec((tc, d), lambda i, c: (c, 0)),
        ],
        out_specs=pl.BlockSpec((tm, d), lambda i, c: (i, 0)),
        out_shape=jax.ShapeDtypeStruct((t, d), jnp.float32),
        scratch_shapes=[pltpu.VMEM((tm, d), jnp.bfloat16)] + _token_tile_scratch(tm, d),
        compiler_params=_compiler_params(_SEQUENTIAL_GRID),
        name="mix_out",
    )(x, gain, o_a, o_b, w_gates, w_gates, p_a, p_b, w_o)


LOG2_E = math.log2(math.e)


def _alibi_slopes(n):
    return jnp.asarray(LOG2_E * 2.0 ** (-8.0 * np.arange(1, n + 1) / n), dtype=jnp.float32)


def _qkv_col_scale():
    scale = np.ones((1, QKV_COLS), np.float32)
    scale[:, :DIFF_WIDTH] = LOG2_E * DIFF_HEAD_DIM ** -0.5
    scale[:, 3 * DIFF_WIDTH:3 * DIFF_WIDTH + MOBA_WIDTH] = LOG2_E * MOBA_HEAD_DIM ** -0.5
    return jnp.asarray(scale)


def kernel(x, g_ffn1, w_ffn1_gu, w_ffn1_down, g_mix, w_in, lam_q1, lam_k1, lam_q2, lam_k2, g_subln, p_a, p_b, w_o, g_ffn2, w_ffn2_gu, w_ffn2_down, g_final):
    b, s, d = x.shape
    assert g_ffn1.shape[0] == 1, "single-layer stack"
    bf16 = jnp.bfloat16
    xt = x.reshape(b * s, d)
    g_final_row = g_final.reshape(1, d)

    x1 = _ffn(xt, g_ffn1, w_ffn1_gu[0], w_ffn1_down[0], g_final_row, final_norm=False)

    qkv = _qkv_proj(x1, g_mix, w_in[0], _qkv_col_scale()).reshape(b, s, QKV_COLS)
    o_a = _diff_attn(qkv, _alibi_slopes(N_HEADS_DIFF), lam_q1, lam_k1, lam_q2, lam_k2, g_subln)
    o_b = _moba_attn(qkv, _alibi_slopes(N_HEADS_MOBA))
    x2 = _mix_out(x1, g_mix, o_a.reshape(b * s, DIFF_WIDTH), o_b.reshape(b * s, MOBA_WIDTH),
                  w_in[0][:, QKV_COLS:].astype(bf16), p_a[0].astype(bf16), p_b[0].astype(bf16),
                  w_o[0].astype(bf16))

    out = _ffn(x2, g_ffn2, w_ffn2_gu[0], w_ffn2_down[0], g_final_row, final_norm=True)
    return out.reshape(b, s, d)
```

```python
import functools
import itertools
import math

import jax
import jax.numpy as jnp
import numpy as np
from jax import lax
from jax.experimental import pallas as pl
from jax.experimental.pallas import tpu as pltpu

D_MODEL = 2048
N_HEADS_DIFF = 8
DIFF_HEAD_DIM = 64
DIFF_WIDTH = N_HEADS_DIFF * 2 * DIFF_HEAD_DIM
N_HEADS_MOBA = 8
MOBA_HEAD_DIM = 128
MOBA_WIDTH = N_HEADS_MOBA * MOBA_HEAD_DIM
MOBA_BLOCK = 256
MOBA_TOPK = 3
D_FF = 5632
RMS_EPS = 1e-6
QKV_COLS = 3 * DIFF_WIDTH + 3 * MOBA_WIDTH
LAM_INIT = 0.8 - 0.6 * math.exp(-0.3 * 0)

LANES = 128
SUBLANES = 8
VMEM_LIMIT_BYTES = 62 * 1024 * 1024
MASK_VALUE = -1e30

FFN_TOKEN_TILE = 1024
FFN_FF_TILE = 512
PROJ_TOKEN_TILE = 2048
PROJ_COL_TILE = 1024
MIX_TOKEN_TILE = 1024
MIX_COL_TILE = 512
ATTN_TILE = 256

_NT_DIMS = (((1,), (1,)), ((), ()))


def _rms_normalize(x, gain):
    ms = jnp.mean(x * x, axis=-1, keepdims=True)
    return x * lax.rsqrt(ms + RMS_EPS) * gain


def _compiler_params(semantics):
    return pltpu.CompilerParams(dimension_semantics=semantics, vmem_limit_bytes=VMEM_LIMIT_BYTES)


def _on_token_tile(x_hbm, x_buf, x_sem, consume):
    i = pl.program_id(0)
    j = pl.program_id(1)
    tm = x_buf.shape[0]

    def copy(tile):
        return pltpu.make_async_copy(x_hbm.at[pl.ds(tile * tm, tm), :], x_buf, x_sem)

    @pl.when((i == 0) & (j == 0))
    def _():
        copy(0).start()

    @pl.when(j == 0)
    def _():
        copy(i).wait()
        consume()

    @pl.when((j == 1) & (i + 1 < pl.num_programs(0)))
    def _():
        copy(i + 1).start()


_X_IN_HBM = pl.BlockSpec(memory_space=pl.ANY)
_SEQUENTIAL_GRID = ("arbitrary", "arbitrary")


def _token_tile_scratch(tm, d):
    return [pltpu.VMEM((tm, d), jnp.float32), pltpu.SemaphoreType.DMA(())]


def _ffn_kernel(x_hbm, g_ref, wg_ref, wu_ref, wd_ref, gf_ref, o_ref, h_ref, x_buf, x_sem,
                *, n_ff_tiles, final_norm):
    f = pl.program_id(1)
    bf16 = jnp.bfloat16

    def start_tile():
        x = x_buf[...]
        h_ref[...] = _rms_normalize(x, g_ref[...]).astype(bf16)
        o_ref[...] = x

    _on_token_tile(x_hbm, x_buf, x_sem, start_tile)

    h = h_ref[...]
    gate = jnp.dot(h, wg_ref[...].astype(bf16), preferred_element_type=jnp.float32)
    up = jnp.dot(h, wu_ref[...].astype(bf16), preferred_element_type=jnp.float32)
    act = (gate * jax.nn.sigmoid(gate) * up * 0.5).astype(bf16)
    o_ref[...] += jnp.dot(act, wd_ref[...].astype(bf16), preferred_element_type=jnp.float32)

    if final_norm:
        @pl.when(f == n_ff_tiles - 1)
        def _():
            o_ref[...] = _rms_normalize(o_ref[...], gf_ref[...])


def _ffn(x, gain, w_gu, w_down, final_gain, *, final_norm):
    t, d = x.shape
    d_ff = w_down.shape[0]
    tm, tf = FFN_TOKEN_TILE, FFN_FF_TILE
    n_ff_tiles = d_ff // tf
    assert t % tm == 0 and d_ff % tf == 0 and n_ff_tiles >= 2
    kern = functools.partial(_ffn_kernel, n_ff_tiles=n_ff_tiles, final_norm=final_norm)
    return pl.pallas_call(
        kern,
        grid=(t // tm, n_ff_tiles),
        in_specs=[
            _X_IN_HBM,
            pl.BlockSpec((1, d), lambda i, f: (0, 0)),
            pl.BlockSpec((d, tf), lambda i, f: (0, f)),
            pl.BlockSpec((d, tf), lambda i, f: (0, f + n_ff_tiles)),
            pl.BlockSpec((tf, d), lambda i, f: (f, 0)),
            pl.BlockSpec((1, d), lambda i, f: (0, 0)),
        ],
        out_specs=pl.BlockSpec((tm, d), lambda i, f: (i, 0)),
        out_shape=jax.ShapeDtypeStruct((t, d), jnp.float32),
        scratch_shapes=[pltpu.VMEM((tm, d), jnp.bfloat16)] + _token_tile_scratch(tm, d),
        compiler_params=_compiler_params(_SEQUENTIAL_GRID),
        name="ffn_final" if final_norm else "ffn",
    )(x, gain, w_gu, w_gu, w_down, final_gain)


def _qkv_proj_kernel(x_hbm, g_ref, w_ref, s_ref, o_ref, h_ref, x_buf, x_sem):
    def start_tile():
        h_ref[...] = _rms_normalize(x_buf[...], g_ref[...]).astype(jnp.bfloat16)

    _on_token_tile(x_hbm, x_buf, x_sem, start_tile)

    acc = jnp.dot(h_ref[...], w_ref[...].astype(jnp.bfloat16), preferred_element_type=jnp.float32)
    o_ref[...] = (acc * s_ref[...]).astype(o_ref.dtype)


def _qkv_proj(x, gain, w_in, col_scale):
    t, d = x.shape
    tm, tn = PROJ_TOKEN_TILE, PROJ_COL_TILE
    assert t % tm == 0 and QKV_COLS % tn == 0 and QKV_COLS // tn >= 2
    return pl.pallas_call(
        _qkv_proj_kernel,
        grid=(t // tm, QKV_COLS // tn),
        in_specs=[
            _X_IN_HBM,
            pl.BlockSpec((1, d), lambda i, j: (0, 0)),
            pl.BlockSpec((d, tn), lambda i, j: (0, j)),
            pl.BlockSpec((1, tn), lambda i, j: (0, j)),
        ],
        out_specs=pl.BlockSpec((tm, tn), lambda i, j: (i, j)),
        out_shape=jax.ShapeDtypeStruct((t, QKV_COLS), jnp.bfloat16),
        scratch_shapes=[pltpu.VMEM((tm, d), jnp.bfloat16)] + _token_tile_scratch(tm, d),
        compiler_params=_compiler_params(_SEQUENTIAL_GRID),
        name="qkv_proj",
    )(x, gain, w_in, col_scale)


ALIBI_TERMS = 3


def _alibi_columns(slope, t):
    lane = lax.broadcasted_iota(jnp.int32, (t, LANES), 1)
    rest = lax.broadcasted_iota(jnp.int32, (t, LANES), 0).astype(jnp.float32) * slope
    k_extra = jnp.zeros((t, LANES), jnp.float32)
    for i in range(ALIBI_TERMS):
        term = rest.astype(jnp.bfloat16).astype(jnp.float32)
        k_extra = jnp.where(lane == i, term, k_extra)
        rest = rest - term
    q_extra = jnp.where(lane < ALIBI_TERMS, 1.0, 0.0)
    return k_extra.astype(jnp.bfloat16), q_extra.astype(jnp.bfloat16)


def _query_shift(slope, t):
    return lax.broadcasted_iota(jnp.int32, (1, t), 1).astype(jnp.float32) * (-slope)


def _causal_mask_t(t):
    kk = lax.broadcasted_iota(jnp.int32, (t, t), 0)
    qq = lax.broadcasted_iota(jnp.int32, (t, t), 1)
    return jnp.where(qq >= kk, 0.0, MASK_VALUE)


def _fold_rows(x, op):
    rows, cols = x.shape
    return op(x.reshape(rows // SUBLANES, SUBLANES, cols), axis=0)


def _score_tiles(k_tile, q, shifts, mask_diag, s_ref, result):
    n, t = len(shifts), q.shape[0]
    cand = None
    for j in range(n):
        sj = lax.dot_general(k_tile(j), q, _NT_DIMS, preferred_element_type=jnp.float32)
        if j == n - 1:
            sj = sj + mask_diag
        s_ref[j * t:(j + 1) * t, :] = sj
        cj = _fold_rows(sj, jnp.max) + shifts[j]
        cand = cj if cand is None else jnp.maximum(cand, cj)
        yield
    result.append(jnp.max(cand, axis=0, keepdims=True))


def _softmax_pv(s_ref, m, shifts, vt_tile, result):
    t = m.shape[1]
    lpart, acc = None, None
    for j in range(len(shifts)):
        sj = s_ref[j * t:(j + 1) * t, :]
        p = jnp.exp2(sj - (m - shifts[j]))
        pj = _fold_rows(p, jnp.sum)
        lpart = pj if lpart is None else lpart + pj
        vt = vt_tile(j)
        pv = jnp.dot(vt, p.astype(vt.dtype), preferred_element_type=jnp.float32)
        acc = pv if acc is None else acc + pv
        yield
    result.append(acc / jnp.sum(lpart, axis=0, keepdims=True))


def _alternate(*steps):
    for _ in itertools.zip_longest(*steps):
        yield


def _run_pipelined(stage_a, stage_b, n, slots):
    ahead = slots - 1
    states = {}

    def run_a(u):
        if u < n:
            steps, states[u] = stage_a(u)
            for _ in steps:
                pass

    for u in range(ahead):
        run_a(u)
    for u in range(n):
        run_a(u + ahead)
        for _ in stage_b(u, states.pop(u)):
            pass


ATTN_HEADS_PER_STEP = 2
ATTN_SCORE_SLOTS = 3


def _diff_attn_kernel(slopes_ref, lq1_ref, lk1_ref, lq2_ref, lk2_ref, gs_ref,
                      q_ref, k_ref, v_ref, o_ref, vt_ref, s_ref):
    t, hw, nh, slots = ATTN_TILE, 2 * DIFF_HEAD_DIM, ATTN_HEADS_PER_STEP, s_ref.shape[0]
    n_tiles = q_ref.shape[1] // t
    mask_diag = _causal_mask_t(t)
    lam = (jnp.exp(jnp.sum(lq1_ref[...] * lk1_ref[...], axis=-1, keepdims=True))
           - jnp.exp(jnp.sum(lq2_ref[...] * lk2_ref[...], axis=-1, keepdims=True))
           + LAM_INIT)
    lane = lax.broadcasted_iota(jnp.int32, (t, LANES), 1)

    def head_setup(hh):
        cols = slice(hh * hw, (hh + 1) * hw)
        slope = slopes_ref[pl.program_id(1) * nh + hh]
        vt_ref[hh] = v_ref[0, :, cols].T
        k_extra, q_extra = _alibi_columns(slope, t)
        q_shift = _query_shift(slope, t)
        return dict(
            cols=cols, q_extra=q_extra,
            shifts=lambda c: [q_shift - slope * float((c - j) * t) for j in range(c + 1)],
            k_tile=lambda j: jnp.concatenate([k_ref[0, j * t:(j + 1) * t, cols], k_extra], axis=1),
            vt_tile=lambda j: vt_ref[hh, :, j * t:(j + 1) * t])

    heads = [head_setup(hh) for hh in range(nh)]
    streams = [(hh, mi) for hh in range(nh) for mi in range(2)]

    def scores(c):
        maxes, steps = [], []
        for hh, mi in streams:
            hd = heads[hh]
            q = q_ref[0, c * t:(c + 1) * t, hd["cols"]]
            keep = (lane < DIFF_HEAD_DIM) if mi == 0 else (lane >= DIFF_HEAD_DIM)
            qm = jnp.where(keep, q, jnp.zeros_like(q))
            maxes.append([])
            steps.append(_score_tiles(hd["k_tile"], jnp.concatenate([qm, hd["q_extra"]], axis=1),
                                      hd["shifts"](c), mask_diag,
                                      s_ref.at[c % slots, 2 * hh + mi], maxes[-1]))
        return _alternate(*steps), maxes

    def finish(c, maxes):
        outs = [[] for _ in streams]
        yield from _alternate(*[
            _softmax_pv(s_ref.at[c % slots, 2 * hh + mi], maxes[si][0], heads[hh]["shifts"](c),
                        heads[hh]["vt_tile"], outs[si])
            for si, (hh, mi) in enumerate(streams)])
        for hh, hd in enumerate(heads):
            o = (outs[2 * hh][0] - lam * outs[2 * hh + 1][0]).T
            o_ref[0, c * t:(c + 1) * t, hd["cols"]] = (
                _rms_normalize(o, gs_ref[...]) * (1.0 - LAM_INIT)).astype(o_ref.dtype)

    _run_pipelined(scores, finish, n_tiles, slots)


def _diff_attn(qkv, slopes, lam_q1, lam_k1, lam_q2, lam_k2, g_subln):
    b, s, _ = qkv.shape
    nh, hw = N_HEADS_DIFF, 2 * DIFF_HEAD_DIM
    hps = ATTN_HEADS_PER_STEP
    w = hps * hw
    assert hw == LANES and s % ATTN_TILE == 0 and nh % hps == 0
    k_blk0, v_blk0 = DIFF_WIDTH // w, 2 * DIFF_WIDTH // w
    lam_spec = pl.BlockSpec((1, DIFF_HEAD_DIM), lambda bi, g: (0, 0))
    return pl.pallas_call(
        _diff_attn_kernel,
        grid=(b, nh // hps),
        in_specs=[
            pl.BlockSpec(memory_space=pltpu.SMEM),
            lam_spec, lam_spec, lam_spec, lam_spec,
            pl.BlockSpec((1, hw), lambda bi, g: (0, 0)),
            pl.BlockSpec((1, s, w), lambda bi, g: (bi, 0, g)),
            pl.BlockSpec((1, s, w), lambda bi, g: (bi, 0, k_blk0 + g)),
            pl.BlockSpec((1, s, w), lambda bi, g: (bi, 0, v_blk0 + g)),
        ],
        out_specs=pl.BlockSpec((1, s, w), lambda bi, g: (bi, 0, g)),
        out_shape=jax.ShapeDtypeStruct((b, s, DIFF_WIDTH), jnp.bfloat16),
        scratch_shapes=[pltpu.VMEM((hps, hw, s), jnp.bfloat16),
                        pltpu.VMEM((ATTN_SCORE_SLOTS, 2 * hps, s, ATTN_TILE), jnp.float32)],
        compiler_params=_compiler_params(("parallel", "parallel")),
        name="diff_attn",
    )(slopes, lam_q1, lam_k1, lam_q2, lam_k2, g_subln, qkv, qkv, qkv)


MOBA_HEADS_PER_STEP = 2
MOBA_SCORE_SLOTS = 3


def _moba_attn_kernel(slopes_ref, q_ref, k_ref, v_ref, *refs, n_blocks, n_sel, n_casts):
    cast_src, o_ref, cast_dst = refs[:n_casts], refs[n_casts], refs[n_casts + 1:2 * n_casts + 1]
    vt_ref, s_ref = refs[2 * n_casts + 1:]
    for src, dst in zip(cast_src, cast_dst):
        dst[...] = src[...].astype(dst.dtype)

    blk, dh, nh, slots = MOBA_BLOCK, MOBA_HEAD_DIM, MOBA_HEADS_PER_STEP, s_ref.shape[0]
    mask_diag = _causal_mask_t(blk)
    blk_id = lax.broadcasted_iota(jnp.int32, (n_blocks, blk), 0)

    def head_setup(hh):
        cols = slice(hh * dh, (hh + 1) * dh)
        slope = slopes_ref[pl.program_id(1) * nh + hh]
        vt_ref[hh] = v_ref[0, :, cols].T
        k_extra, q_extra = _alibi_columns(slope, blk)
        kmean = jnp.mean(k_ref[0, :, cols].astype(jnp.float32).reshape(n_blocks, blk, dh), axis=1)
        kmean = jnp.concatenate([kmean, jnp.zeros((LANES - n_blocks, dh), jnp.float32)], axis=0)
        kmean_hi = kmean.astype(jnp.bfloat16)
        return dict(
            cols=cols, slope=slope, q_extra=q_extra, q_shift=_query_shift(slope, blk),
            kmean_hi=kmean_hi, kmean_lo=(kmean - kmean_hi.astype(jnp.float32)).astype(jnp.bfloat16),
            k_tile=lambda j: jnp.concatenate([k_ref[0, j * blk:(j + 1) * blk, cols], k_extra], axis=1),
            vt_tile=lambda j: vt_ref[hh, :, j * blk:(j + 1) * blk])

    heads = [head_setup(hh) for hh in range(nh)]

    def head_scores(c, hh, hd):
        q = q_ref[0, c * blk:(c + 1) * blk, hd["cols"]]
        g = (lax.dot_general(hd["kmean_hi"], q, _NT_DIMS, preferred_element_type=jnp.float32)
             + lax.dot_general(hd["kmean_lo"], q, _NT_DIMS, preferred_element_type=jnp.float32))[:n_blocks]
        past = blk_id < c
        g = jnp.where(past, g, -jnp.inf)
        rank = jnp.zeros(g.shape, jnp.int32)
        for r in range(1, n_blocks):
            other = pltpu.roll(g, r, 0)
            rank += jnp.where(blk_id >= r, (other >= g).astype(jnp.int32), (other > g).astype(jnp.int32))
        keep = (past & (rank < n_sel)) | (blk_id == c)
        sel_bias = jnp.where(keep, 0.0, MASK_VALUE)
        shifts = [sel_bias[j:j + 1, :] + (hd["q_shift"] - hd["slope"] * float((c - j) * blk))
                  for j in range(c + 1)]
        q_aug = jnp.concatenate([q, hd["q_extra"]], axis=1)
        m = []
        steps = _score_tiles(hd["k_tile"], q_aug, shifts, mask_diag, s_ref.at[c % slots, hh], m)
        return steps, (m, shifts)

    def scores(c):
        per_head = [head_scores(c, hh, hd) for hh, hd in enumerate(heads)]
        return _alternate(*[steps for steps, _ in per_head]), [state for _, state in per_head]

    def finish(c, states):
        outs = [[] for _ in heads]
        yield from _alternate(*[
            _softmax_pv(s_ref.at[c % slots, hh], m[0], shifts, heads[hh]["vt_tile"], outs[hh])
            for hh, (m, shifts) in enumerate(states)])
        for hh, hd in enumerate(heads):
            o_ref[0, c * blk:(c + 1) * blk, hd["cols"]] = outs[hh][0].T.astype(o_ref.dtype)

    _run_pipelined(scores, finish, n_blocks, slots)


def _moba_attn(qkv, slopes, w_in, row_weights):
    b, s, _ = qkv.shape
    nh, dh, blk = N_HEADS_MOBA, MOBA_HEAD_DIM, MOBA_BLOCK
    assert dh == LANES and s % blk == 0
    n_blocks = s // blk
    assert n_blocks == SUBLANES, "block ranking uses one vreg row per MoBA block"
    n_sel = min(MOBA_TOPK, n_blocks - 1)
    hps = MOBA_HEADS_PER_STEP
    w = hps * dh
    assert nh % hps == 0 and (3 * DIFF_WIDTH) % w == 0 and MOBA_WIDTH % w == 0
    q_blk0 = 3 * DIFF_WIDTH // w
    k_blk0, v_blk0 = q_blk0 + nh // hps, q_blk0 + 2 * nh // hps

    n_g = nh // hps
    n_steps = b * n_g
    step = lambda bi, g: bi * n_g + g
    d, gate_cols = w_in.shape[0], w_in.shape[1] - QKV_COLS
    gc = gate_cols // n_steps
    assert gate_cols % n_steps == 0 and gc % LANES == 0 and QKV_COLS % gc == 0
    cast_in = [pl.BlockSpec((d, gc), lambda bi, g: (0, QKV_COLS // gc + step(bi, g)))]
    cast_out = [pl.BlockSpec((d, gc), lambda bi, g: (0, step(bi, g)))]
    cast_shapes = [jax.ShapeDtypeStruct((d, gate_cols), jnp.bfloat16)]
    for m in row_weights:
        rows = m.shape[0] // n_steps
        assert m.shape[0] % n_steps == 0 and rows % (2 * SUBLANES) == 0
        spec = pl.BlockSpec((rows, m.shape[1]), lambda bi, g: (step(bi, g), 0))
        cast_in.append(spec)
        cast_out.append(spec)
        cast_shapes.append(jax.ShapeDtypeStruct(m.shape, jnp.bfloat16))

    kern = functools.partial(_moba_attn_kernel, n_blocks=n_blocks, n_sel=n_sel, n_casts=len(cast_in))
    o_b, w_gates, *row_weights_bf = pl.pallas_call(
        kern,
        grid=(b, n_g),
        in_specs=[
            pl.BlockSpec(memory_space=pltpu.SMEM),
            pl.BlockSpec((1, s, w), lambda bi, g: (bi, 0, q_blk0 + g)),
            pl.BlockSpec((1, s, w), lambda bi, g: (bi, 0, k_blk0 + g)),
            pl.BlockSpec((1, s, w), lambda bi, g: (bi, 0, v_blk0 + g)),
        ] + cast_in,
        out_specs=[pl.BlockSpec((1, s, w), lambda bi, g: (bi, 0, g))] + cast_out,
        out_shape=[jax.ShapeDtypeStruct((b, s, MOBA_WIDTH), jnp.bfloat16)] + cast_shapes,
        scratch_shapes=[pltpu.VMEM((hps, dh, s), jnp.bfloat16),
                        pltpu.VMEM((MOBA_SCORE_SLOTS, hps, s, blk), jnp.float32)],
        compiler_params=_compiler_params(("parallel", "parallel")),
        name="moba_attn",
    )(slopes, qkv, qkv, qkv, w_in, *row_weights)
    return o_b, w_gates, row_weights_bf


def _mix_out_kernel(x_hbm, g_ref, oa_ref, ob_ref, wga_ref, wgb_ref, pa_ref, pb_ref, wo_ref, o_ref,
                    h_ref, x_buf, x_sem):
    def start_tile():
        x = x_buf[...]
        h_ref[...] = _rms_normalize(x, g_ref[...]).astype(jnp.bfloat16)
        o_ref[...] = x

    _on_token_tile(x_hbm, x_buf, x_sem, start_tile)

    h = h_ref[...]
    gate_a = jnp.dot(h, wga_ref[...], preferred_element_type=jnp.float32)
    gate_b = jnp.dot(h, wgb_ref[...], preferred_element_type=jnp.float32)
    proj_a = jnp.dot(oa_ref[...], pa_ref[...], preferred_element_type=jnp.float32)
    proj_b = jnp.dot(ob_ref[...], pb_ref[...], preferred_element_type=jnp.float32)
    merged = jax.nn.sigmoid(gate_a) * proj_a + jax.nn.sigmoid(gate_b) * proj_b
    o_ref[...] += jnp.dot(merged.astype(jnp.bfloat16), wo_ref[...], preferred_element_type=jnp.float32)


def _mix_out(x, gain, o_a, o_b, w_gates, p_a, p_b, w_o):
    t, d = x.shape
    tm, tc = MIX_TOKEN_TILE, MIX_COL_TILE
    assert t % tm == 0 and d % tc == 0 and d // tc >= 2
    ga_blk0, gb_blk0 = 0, d // tc
    return pl.pallas_call(
        _mix_out_kernel,
        grid=(t // tm, d // tc),
        in_specs=[
            _X_IN_HBM,
            pl.BlockSpec((1, d), lambda i, c: (0, 0)),
            pl.BlockSpec((tm, DIFF_WIDTH), lambda i, c: (i, 0)),
            pl.BlockSpec((tm, MOBA_WIDTH), lambda i, c: (i, 0)),
            pl.BlockSpec((d, tc), lambda i, c: (0, ga_blk0 + c)),
            pl.BlockSpec((d, tc), lambda i, c: (0, gb_blk0 + c)),
            pl.BlockSpec((DIFF_WIDTH, tc), lambda i, c: (0, c)),
            pl.BlockSpec((MOBA_WIDTH, tc), lambda i, c: (0, c)),
            pl.BlockSpec((tc, d), lambda i, c: (c, 0)),
        ],
        out_specs=pl.BlockSpec((tm, d), lambda i, c: (i, 0)),
        out_shape=jax.ShapeDtypeStruct((t, d), jnp.float32),
        scratch_shapes=[pltpu.VMEM((tm, d), jnp.bfloat16)] + _token_tile_scratch(tm, d),
        compiler_params=_compiler_params(_SEQUENTIAL_GRID),
        name="mix_out",
    )(x, gain, o_a, o_b, w_gates, w_gates, p_a, p_b, w_o)


LOG2_E = math.log2(math.e)


def _alibi_slopes(n):
    return jnp.asarray(LOG2_E * 2.0 ** (-8.0 * np.arange(1, n + 1) / n), dtype=jnp.float32)


def _qkv_col_scale():
    scale = np.ones((1, QKV_COLS), np.float32)
    scale[:, :DIFF_WIDTH] = LOG2_E * DIFF_HEAD_DIM ** -0.5
    scale[:, 3 * DIFF_WIDTH:3 * DIFF_WIDTH + MOBA_WIDTH] = LOG2_E * MOBA_HEAD_DIM ** -0.5
    return jnp.asarray(scale)


def kernel(x, g_ffn1, w_ffn1_gu, w_ffn1_down, g_mix, w_in, lam_q1, lam_k1, lam_q2, lam_k2, g_subln, p_a, p_b, w_o, g_ffn2, w_ffn2_gu, w_ffn2_down, g_final):
    b, s, d = x.shape
    assert g_ffn1.shape[0] == 1, "single-layer stack"
    xt = x.reshape(b * s, d)
    g_final_row = g_final.reshape(1, d)

    x1 = _ffn(xt, g_ffn1, w_ffn1_gu[0], w_ffn1_down[0], g_final_row, final_norm=False)

    qkv = _qkv_proj(x1, g_mix, w_in[0], _qkv_col_scale()).reshape(b, s, QKV_COLS)
    o_a = _diff_attn(qkv, _alibi_slopes(N_HEADS_DIFF), lam_q1, lam_k1, lam_q2, lam_k2, g_subln)
    o_b, w_gates, (p_a_bf, p_b_bf, w_o_bf) = _moba_attn(
        qkv, _alibi_slopes(N_HEADS_MOBA), w_in[0], (p_a[0], p_b[0], w_o[0]))
    x2 = _mix_out(x1, g_mix, o_a.reshape(b * s, DIFF_WIDTH), o_b.reshape(b * s, MOBA_WIDTH),
                  w_gates, p_a_bf, p_b_bf, w_o_bf)

    out = _ffn(x2, g_ffn2, w_ffn2_gu[0], w_ffn2_down[0], g_final_row, final_norm=True)
    return out.reshape(b, s, d)
```

```python
import functools
import itertools
import math

import jax
import jax.numpy as jnp
import numpy as np
from jax import lax
from jax.experimental import pallas as pl
from jax.experimental.pallas import tpu as pltpu

D_MODEL = 2048
N_HEADS_DIFF = 8
DIFF_HEAD_DIM = 64
DIFF_WIDTH = N_HEADS_DIFF * 2 * DIFF_HEAD_DIM
N_HEADS_MOBA = 8
MOBA_HEAD_DIM = 128
MOBA_WIDTH = N_HEADS_MOBA * MOBA_HEAD_DIM
MOBA_BLOCK = 256
MOBA_TOPK = 3
D_FF = 5632
RMS_EPS = 1e-6
QKV_COLS = 3 * DIFF_WIDTH + 3 * MOBA_WIDTH
LAM_INIT = 0.8 - 0.6 * math.exp(-0.3 * 0)

LANES = 128
SUBLANES = 8
VMEM_LIMIT_BYTES = 63 * 1024 * 1024
MASK_VALUE = -1e30

FFN_TOKEN_TILE = 1024
FFN_FF_TILE = 512
PROJ_TOKEN_TILE = 2048
PROJ_COL_TILE = 1024
MIX_TOKEN_TILE = 1024
MIX_COL_TILE = 512
ATTN_TILE = 256

_NT_DIMS = (((1,), (1,)), ((), ()))


def _rms_normalize(x, gain):
    ms = jnp.mean(x * x, axis=-1, keepdims=True)
    return x * lax.rsqrt(ms + RMS_EPS) * gain


def _compiler_params(semantics):
    return pltpu.CompilerParams(dimension_semantics=semantics, vmem_limit_bytes=VMEM_LIMIT_BYTES)


def _on_token_tile(x_hbm, x_buf, x_sem, consume):
    i = pl.program_id(0)
    j = pl.program_id(1)
    tm = x_buf.shape[0]

    def copy(tile):
        return pltpu.make_async_copy(x_hbm.at[pl.ds(tile * tm, tm), :], x_buf, x_sem)

    @pl.when((i == 0) & (j == 0))
    def _():
        copy(0).start()

    @pl.when(j == 0)
    def _():
        copy(i).wait()
        consume()

    @pl.when((j == 1) & (i + 1 < pl.num_programs(0)))
    def _():
        copy(i + 1).start()


_X_IN_HBM = pl.BlockSpec(memory_space=pl.ANY)
_SEQUENTIAL_GRID = ("arbitrary", "arbitrary")


def _token_tile_scratch(tm, d):
    return [pltpu.VMEM((tm, d), jnp.float32), pltpu.SemaphoreType.DMA(())]


def _ffn_kernel(x_hbm, g_ref, wg_ref, wu_ref, wd_ref, gf_ref, o_ref, h_ref, x_buf, x_sem,
                *, n_ff_tiles, final_norm):
    f = pl.program_id(1)
    bf16 = jnp.bfloat16
    last = n_ff_tiles - 1

    def half_swiglu():
        h = h_ref[...]
        gate = jnp.dot(h, wg_ref[...].astype(bf16), preferred_element_type=jnp.float32)
        up = jnp.dot(h, wu_ref[...].astype(bf16), preferred_element_type=jnp.float32)
        act = (gate * jax.nn.sigmoid(gate) * up * 0.5).astype(bf16)
        return jnp.dot(act, wd_ref[...].astype(bf16), preferred_element_type=jnp.float32)

    def first_step():
        x = x_buf[...]
        h_ref[...] = _rms_normalize(x, g_ref[...]).astype(bf16)
        o_ref[...] = x + half_swiglu()

    _on_token_tile(x_hbm, x_buf, x_sem, first_step)

    @pl.when((f > 0) & (f < last) if final_norm else (f > 0))
    def _():
        o_ref[...] += half_swiglu()

    if final_norm:
        @pl.when(f == last)
        def _():
            o_ref[...] = _rms_normalize(o_ref[...] + half_swiglu(), gf_ref[...])


def _ffn(x, gain, w_gu, w_down, final_gain, *, final_norm):
    t, d = x.shape
    d_ff = w_down.shape[0]
    tm, tf = FFN_TOKEN_TILE, FFN_FF_TILE
    n_ff_tiles = d_ff // tf
    assert t % tm == 0 and d_ff % tf == 0 and n_ff_tiles >= 2
    kern = functools.partial(_ffn_kernel, n_ff_tiles=n_ff_tiles, final_norm=final_norm)
    return pl.pallas_call(
        kern,
        grid=(t // tm, n_ff_tiles),
        in_specs=[
            _X_IN_HBM,
            pl.BlockSpec((1, d), lambda i, f: (0, 0)),
            pl.BlockSpec((d, tf), lambda i, f: (0, f)),
            pl.BlockSpec((d, tf), lambda i, f: (0, f + n_ff_tiles)),
            pl.BlockSpec((tf, d), lambda i, f: (f, 0)),
            pl.BlockSpec((1, d), lambda i, f: (0, 0)),
        ],
        out_specs=pl.BlockSpec((tm, d), lambda i, f: (i, 0)),
        out_shape=jax.ShapeDtypeStruct((t, d), jnp.float32),
        scratch_shapes=[pltpu.VMEM((tm, d), jnp.bfloat16)] + _token_tile_scratch(tm, d),
        compiler_params=_compiler_params(_SEQUENTIAL_GRID),
        name="ffn_final" if final_norm else "ffn",
    )(x, gain, w_gu, w_gu, w_down, final_gain)


def _qkv_proj_kernel(x_hbm, g_ref, w_ref, s_ref, o_ref, h_ref, x_buf, x_sem):
    def start_tile():
        h_ref[...] = _rms_normalize(x_buf[...], g_ref[...]).astype(jnp.bfloat16)

    _on_token_tile(x_hbm, x_buf, x_sem, start_tile)

    acc = jnp.dot(h_ref[...], w_ref[...].astype(jnp.bfloat16), preferred_element_type=jnp.float32)
    o_ref[...] = (acc * s_ref[...]).astype(o_ref.dtype)


def _qkv_proj(x, gain, w_in, col_scale):
    t, d = x.shape
    tm, tn = PROJ_TOKEN_TILE, PROJ_COL_TILE
    assert t % tm == 0 and QKV_COLS % tn == 0 and QKV_COLS // tn >= 2
    return pl.pallas_call(
        _qkv_proj_kernel,
        grid=(t // tm, QKV_COLS // tn),
        in_specs=[
            _X_IN_HBM,
            pl.BlockSpec((1, d), lambda i, j: (0, 0)),
            pl.BlockSpec((d, tn), lambda i, j: (0, j)),
            pl.BlockSpec((1, tn), lambda i, j: (0, j)),
        ],
        out_specs=pl.BlockSpec((tm, tn), lambda i, j: (i, j)),
        out_shape=jax.ShapeDtypeStruct((t, QKV_COLS), jnp.bfloat16),
        scratch_shapes=[pltpu.VMEM((tm, d), jnp.bfloat16)] + _token_tile_scratch(tm, d),
        compiler_params=_compiler_params(_SEQUENTIAL_GRID),
        name="qkv_proj",
    )(x, gain, w_in, col_scale)


ALIBI_TERMS = 3


def _alibi_columns(slope, t):
    lane = lax.broadcasted_iota(jnp.int32, (t, LANES), 1)
    rest = lax.broadcasted_iota(jnp.int32, (t, LANES), 0).astype(jnp.float32) * slope
    k_extra = jnp.zeros((t, LANES), jnp.float32)
    for i in range(ALIBI_TERMS):
        term = rest.astype(jnp.bfloat16).astype(jnp.float32)
        k_extra = jnp.where(lane == i, term, k_extra)
        rest = rest - term
    q_extra = jnp.where(lane < ALIBI_TERMS, 1.0, 0.0)
    return k_extra.astype(jnp.bfloat16), q_extra.astype(jnp.bfloat16)


def _query_shift(slope, t):
    return lax.broadcasted_iota(jnp.int32, (1, t), 1).astype(jnp.float32) * (-slope)


def _causal_mask_t(t):
    kk = lax.broadcasted_iota(jnp.int32, (t, t), 0)
    qq = lax.broadcasted_iota(jnp.int32, (t, t), 1)
    return jnp.where(qq >= kk, 0.0, MASK_VALUE)


def _fold_rows(x, op):
    rows, cols = x.shape
    return op(x.reshape(rows // SUBLANES, SUBLANES, cols), axis=0)


def _score_tiles(k_tile, q, shifts, mask_diag, s_ref, result):
    n, t = len(shifts), q.shape[0]
    cand = None
    for j in range(n):
        sj = lax.dot_general(k_tile(j), q, _NT_DIMS, preferred_element_type=jnp.float32)
        if j == n - 1:
            sj = sj + mask_diag
        s_ref[j * t:(j + 1) * t, :] = sj
        cj = _fold_rows(sj, jnp.max) + shifts[j]
        cand = cj if cand is None else jnp.maximum(cand, cj)
        yield
    result.append(jnp.max(cand, axis=0, keepdims=True))


def _softmax_pv(s_ref, m, shifts, vt_tile, result):
    t = m.shape[1]
    lpart, acc = None, None
    for j in range(len(shifts)):
        sj = s_ref[j * t:(j + 1) * t, :]
        p = jnp.exp2(sj - (m - shifts[j]))
        pj = _fold_rows(p, jnp.sum)
        lpart = pj if lpart is None else lpart + pj
        vt = vt_tile(j)
        pv = jnp.dot(vt, p.astype(vt.dtype), preferred_element_type=jnp.float32)
        acc = pv if acc is None else acc + pv
        yield
    result.append(acc / jnp.sum(lpart, axis=0, keepdims=True))


def _alternate(*steps):
    for _ in itertools.zip_longest(*steps):
        yield


def _run_pipelined(stage_a, stage_b, n, slots):
    ahead = slots - 1
    states = {}

    def run_a(u):
        if u < n:
            steps, states[u] = stage_a(u)
            for _ in steps:
                pass

    for u in range(ahead):
        run_a(u)
    for u in range(n):
        run_a(u + ahead)
        for _ in stage_b(u, states.pop(u)):
            pass


ATTN_HEADS_PER_STEP = 2
ATTN_SCORE_SLOTS = 3


def _diff_attn_kernel(slopes_ref, lq1_ref, lk1_ref, lq2_ref, lk2_ref, gs_ref,
                      q_ref, k_ref, v_ref, o_ref, vt_ref, s_ref):
    t, hw, nh, slots = ATTN_TILE, 2 * DIFF_HEAD_DIM, ATTN_HEADS_PER_STEP, s_ref.shape[0]
    n_tiles = q_ref.shape[1] // t
    mask_diag = _causal_mask_t(t)
    lam = (jnp.exp(jnp.sum(lq1_ref[...] * lk1_ref[...], axis=-1, keepdims=True))
           - jnp.exp(jnp.sum(lq2_ref[...] * lk2_ref[...], axis=-1, keepdims=True))
           + LAM_INIT)
    lane = lax.broadcasted_iota(jnp.int32, (t, LANES), 1)

    def head_setup(hh):
        cols = slice(hh * hw, (hh + 1) * hw)
        slope = slopes_ref[pl.program_id(1) * nh + hh]
        vt_ref[hh] = v_ref[0, :, cols].T
        k_extra, q_extra = _alibi_columns(slope, t)
        q_shift = _query_shift(slope, t)
        return dict(
            cols=cols, q_extra=q_extra,
            shifts=lambda c: [q_shift - slope * float((c - j) * t) for j in range(c + 1)],
            k_tile=lambda j: jnp.concatenate([k_ref[0, j * t:(j + 1) * t, cols], k_extra], axis=1),
            vt_tile=lambda j: vt_ref[hh, :, j * t:(j + 1) * t])

    heads = [head_setup(hh) for hh in range(nh)]
    streams = [(hh, mi) for hh in range(nh) for mi in range(2)]

    def scores(c):
        maxes, steps = [], []
        for hh, mi in streams:
            hd = heads[hh]
            q = q_ref[0, c * t:(c + 1) * t, hd["cols"]]
            keep = (lane < DIFF_HEAD_DIM) if mi == 0 else (lane >= DIFF_HEAD_DIM)
            qm = jnp.where(keep, q, jnp.zeros_like(q))
            maxes.append([])
            steps.append(_score_tiles(hd["k_tile"], jnp.concatenate([qm, hd["q_extra"]], axis=1),
                                      hd["shifts"](c), mask_diag,
                                      s_ref.at[c % slots, 2 * hh + mi], maxes[-1]))
        return _alternate(*steps), maxes

    def finish(c, maxes):
        outs = [[] for _ in streams]
        yield from _alternate(*[
            _softmax_pv(s_ref.at[c % slots, 2 * hh + mi], maxes[si][0], heads[hh]["shifts"](c),
                        heads[hh]["vt_tile"], outs[si])
            for si, (hh, mi) in enumerate(streams)])
        for hh, hd in enumerate(heads):
            o = (outs[2 * hh][0] - lam * outs[2 * hh + 1][0]).T
            o_ref[0, c * t:(c + 1) * t, hd["cols"]] = (
                _rms_normalize(o, gs_ref[...]) * (1.0 - LAM_INIT)).astype(o_ref.dtype)

    _run_pipelined(scores, finish, n_tiles, slots)


def _diff_attn(qkv, slopes, lam_q1, lam_k1, lam_q2, lam_k2, g_subln):
    b, s, _ = qkv.shape
    nh, hw = N_HEADS_DIFF, 2 * DIFF_HEAD_DIM
    hps = ATTN_HEADS_PER_STEP
    w = hps * hw
    assert hw == LANES and s % ATTN_TILE == 0 and nh % hps == 0
    k_blk0, v_blk0 = DIFF_WIDTH // w, 2 * DIFF_WIDTH // w
    lam_spec = pl.BlockSpec((1, DIFF_HEAD_DIM), lambda bi, g: (0, 0))
    return pl.pallas_call(
        _diff_attn_kernel,
        grid=(b, nh // hps),
        in_specs=[
            pl.BlockSpec(memory_space=pltpu.SMEM),
            lam_spec, lam_spec, lam_spec, lam_spec,
            pl.BlockSpec((1, hw), lambda bi, g: (0, 0)),
            pl.BlockSpec((1, s, w), lambda bi, g: (bi, 0, g)),
            pl.BlockSpec((1, s, w), lambda bi, g: (bi, 0, k_blk0 + g)),
            pl.BlockSpec((1, s, w), lambda bi, g: (bi, 0, v_blk0 + g)),
        ],
        out_specs=pl.BlockSpec((1, s, w), lambda bi, g: (bi, 0, g)),
        out_shape=jax.ShapeDtypeStruct((b, s, DIFF_WIDTH), jnp.bfloat16),
        scratch_shapes=[pltpu.VMEM((hps, hw, s), jnp.bfloat16),
                        pltpu.VMEM((ATTN_SCORE_SLOTS, 2 * hps, s, ATTN_TILE), jnp.float32)],
        compiler_params=_compiler_params(("parallel", "parallel")),
        name="diff_attn",
    )(slopes, lam_q1, lam_k1, lam_q2, lam_k2, g_subln, qkv, qkv, qkv)


MOBA_HEADS_PER_STEP = 2
MOBA_SCORE_SLOTS = 3


def _moba_attn_kernel(slopes_ref, q_ref, k_ref, v_ref, *refs, n_blocks, n_sel, n_casts):
    cast_src, o_ref, cast_dst = refs[:n_casts], refs[n_casts], refs[n_casts + 1:2 * n_casts + 1]
    vt_ref, s_ref = refs[2 * n_casts + 1:]
    for src, dst in zip(cast_src, cast_dst):
        dst[...] = src[...].astype(dst.dtype)

    blk, dh, nh, slots = MOBA_BLOCK, MOBA_HEAD_DIM, MOBA_HEADS_PER_STEP, s_ref.shape[0]
    mask_diag = _causal_mask_t(blk)
    blk_id = lax.broadcasted_iota(jnp.int32, (n_blocks, blk), 0)

    def head_setup(hh):
        cols = slice(hh * dh, (hh + 1) * dh)
        slope = slopes_ref[pl.program_id(1) * nh + hh]
        vt_ref[hh] = v_ref[0, :, cols].T
        k_extra, q_extra = _alibi_columns(slope, blk)
        kmean = jnp.mean(k_ref[0, :, cols].astype(jnp.float32).reshape(n_blocks, blk, dh), axis=1)
        kmean = jnp.concatenate([kmean, jnp.zeros((LANES - n_blocks, dh), jnp.float32)], axis=0)
        kmean_hi = kmean.astype(jnp.bfloat16)
        return dict(
            cols=cols, slope=slope, q_extra=q_extra, q_shift=_query_shift(slope, blk),
            kmean_hi=kmean_hi, kmean_lo=(kmean - kmean_hi.astype(jnp.float32)).astype(jnp.bfloat16),
            k_tile=lambda j: jnp.concatenate([k_ref[0, j * blk:(j + 1) * blk, cols], k_extra], axis=1),
            vt_tile=lambda j: vt_ref[hh, :, j * blk:(j + 1) * blk])

    heads = [head_setup(hh) for hh in range(nh)]

    def head_scores(c, hh, hd):
        q = q_ref[0, c * blk:(c + 1) * blk, hd["cols"]]
        g = (lax.dot_general(hd["kmean_hi"], q, _NT_DIMS, preferred_element_type=jnp.float32)
             + lax.dot_general(hd["kmean_lo"], q, _NT_DIMS, preferred_element_type=jnp.float32))[:n_blocks]
        past = blk_id < c
        g = jnp.where(past, g, -jnp.inf)
        rank = jnp.zeros(g.shape, jnp.int32)
        for r in range(1, n_blocks):
            other = pltpu.roll(g, r, 0)
            rank += jnp.where(blk_id >= r, (other >= g).astype(jnp.int32), (other > g).astype(jnp.int32))
        keep = (past & (rank < n_sel)) | (blk_id == c)
        sel_bias = jnp.where(keep, 0.0, MASK_VALUE)
        shifts = [sel_bias[j:j + 1, :] + (hd["q_shift"] - hd["slope"] * float((c - j) * blk))
                  for j in range(c + 1)]
        q_aug = jnp.concatenate([q, hd["q_extra"]], axis=1)
        m = []
        steps = _score_tiles(hd["k_tile"], q_aug, shifts, mask_diag, s_ref.at[c % slots, hh], m)
        return steps, (m, shifts)

    def scores(c):
        per_head = [head_scores(c, hh, hd) for hh, hd in enumerate(heads)]
        return _alternate(*[steps for steps, _ in per_head]), [state for _, state in per_head]

    def finish(c, states):
        outs = [[] for _ in heads]
        yield from _alternate(*[
            _softmax_pv(s_ref.at[c % slots, hh], m[0], shifts, heads[hh]["vt_tile"], outs[hh])
            for hh, (m, shifts) in enumerate(states)])
        for hh, hd in enumerate(heads):
            o_ref[0, c * blk:(c + 1) * blk, hd["cols"]] = outs[hh][0].T.astype(o_ref.dtype)

    _run_pipelined(scores, finish, n_blocks, slots)


def _moba_attn(qkv, slopes, w_in, row_weights):
    b, s, _ = qkv.shape
    nh, dh, blk = N_HEADS_MOBA, MOBA_HEAD_DIM, MOBA_BLOCK
    assert dh == LANES and s % blk == 0
    n_blocks = s // blk
    assert n_blocks == SUBLANES, "block ranking uses one vreg row per MoBA block"
    n_sel = min(MOBA_TOPK, n_blocks - 1)
    hps = MOBA_HEADS_PER_STEP
    w = hps * dh
    assert nh % hps == 0 and (3 * DIFF_WIDTH) % w == 0 and MOBA_WIDTH % w == 0
    q_blk0 = 3 * DIFF_WIDTH // w
    k_blk0, v_blk0 = q_blk0 + nh // hps, q_blk0 + 2 * nh // hps

    n_g = nh // hps
    n_steps = b * n_g
    step = lambda bi, g: bi * n_g + g
    d, gate_cols = w_in.shape[0], w_in.shape[1] - QKV_COLS
    gc = gate_cols // n_steps
    assert gate_cols % n_steps == 0 and gc % LANES == 0 and QKV_COLS % gc == 0
    cast_in = [pl.BlockSpec((d, gc), lambda bi, g: (0, QKV_COLS // gc + step(bi, g)))]
    cast_out = [pl.BlockSpec((d, gc), lambda bi, g: (0, step(bi, g)))]
    cast_shapes = [jax.ShapeDtypeStruct((d, gate_cols), jnp.bfloat16)]
    for m in row_weights:
        rows = m.shape[0] // n_steps
        assert m.shape[0] % n_steps == 0 and rows % (2 * SUBLANES) == 0
        spec = pl.BlockSpec((rows, m.shape[1]), lambda bi, g: (step(bi, g), 0))
        cast_in.append(spec)
        cast_out.append(spec)
        cast_shapes.append(jax.ShapeDtypeStruct(m.shape, jnp.bfloat16))

    kern = functools.partial(_moba_attn_kernel, n_blocks=n_blocks, n_sel=n_sel, n_casts=len(cast_in))
    o_b, w_gates, *row_weights_bf = pl.pallas_call(
        kern,
        grid=(b, n_g),
        in_specs=[
            pl.BlockSpec(memory_space=pltpu.SMEM),
            pl.BlockSpec((1, s, w), lambda bi, g: (bi, 0, q_blk0 + g)),
            pl.BlockSpec((1, s, w), lambda bi, g: (bi, 0, k_blk0 + g)),
            pl.BlockSpec((1, s, w), lambda bi, g: (bi, 0, v_blk0 + g)),
        ] + cast_in,
        out_specs=[pl.BlockSpec((1, s, w), lambda bi, g: (bi, 0, g))] + cast_out,
        out_shape=[jax.ShapeDtypeStruct((b, s, MOBA_WIDTH), jnp.bfloat16)] + cast_shapes,
        scratch_shapes=[pltpu.VMEM((hps, dh, s), jnp.bfloat16),
                        pltpu.VMEM((MOBA_SCORE_SLOTS, hps, s, blk), jnp.float32)],
        compiler_params=_compiler_params(("parallel", "parallel")),
        name="moba_attn",
    )(slopes, qkv, qkv, qkv, w_in, *row_weights)
    return o_b, w_gates, row_weights_bf


def _mix_out_kernel(x_hbm, g_ref, oa_ref, ob_ref, wga_ref, wgb_ref, pa_ref, pb_ref, wo_ref, o_ref,
                    h_ref, x_buf, x_sem):
    def start_tile():
        x = x_buf[...]
        h_ref[...] = _rms_normalize(x, g_ref[...]).astype(jnp.bfloat16)
        o_ref[...] = x

    _on_token_tile(x_hbm, x_buf, x_sem, start_tile)

    h = h_ref[...]
    gate_a = jnp.dot(h, wga_ref[...], preferred_element_type=jnp.float32)
    gate_b = jnp.dot(h, wgb_ref[...], preferred_element_type=jnp.float32)
    proj_a = jnp.dot(oa_ref[...], pa_ref[...], preferred_element_type=jnp.float32)
    proj_b = jnp.dot(ob_ref[...], pb_ref[...], preferred_element_type=jnp.float32)
    merged = jax.nn.sigmoid(gate_a) * proj_a + jax.nn.sigmoid(gate_b) * proj_b
    o_ref[...] += jnp.dot(merged.astype(jnp.bfloat16), wo_ref[...], preferred_element_type=jnp.float32)


def _mix_out(x, gain, o_a, o_b, w_gates, p_a, p_b, w_o):
    t, d = x.shape
    tm, tc = MIX_TOKEN_TILE, MIX_COL_TILE
    assert t % tm == 0 and d % tc == 0 and d // tc >= 2
    ga_blk0, gb_blk0 = 0, d // tc
    return pl.pallas_call(
        _mix_out_kernel,
        grid=(t // tm, d // tc),
        in_specs=[
            _X_IN_HBM,
            pl.BlockSpec((1, d), lambda i, c: (0, 0)),
            pl.BlockSpec((tm, DIFF_WIDTH), lambda i, c: (i, 0)),
            pl.BlockSpec((tm, MOBA_WIDTH), lambda i, c: (i, 0)),
            pl.BlockSpec((d, tc), lambda i, c: (0, ga_blk0 + c)),
            pl.BlockSpec((d, tc), lambda i, c: (0, gb_blk0 + c)),
            pl.BlockSpec((DIFF_WIDTH, tc), lambda i, c: (0, c)),
            pl.BlockSpec((MOBA_WIDTH, tc), lambda i, c: (0, c)),
            pl.BlockSpec((tc, d), lambda i, c: (c, 0)),
        ],
        out_specs=pl.BlockSpec((tm, d), lambda i, c: (i, 0)),
        out_shape=jax.ShapeDtypeStruct((t, d), jnp.float32),
        scratch_shapes=[pltpu.VMEM((tm, d), jnp.bfloat16)] + _token_tile_scratch(tm, d),
        compiler_params=_compiler_params(_SEQUENTIAL_GRID),
        name="mix_out",
    )(x, gain, o_a, o_b, w_gates, w_gates, p_a, p_b, w_o)


LOG2_E = math.log2(math.e)


def _alibi_slopes(n):
    return jnp.asarray(LOG2_E * 2.0 ** (-8.0 * np.arange(1, n + 1) / n), dtype=jnp.float32)


def _qkv_col_scale():
    scale = np.ones((1, QKV_COLS), np.float32)
    scale[:, :DIFF_WIDTH] = LOG2_E * DIFF_HEAD_DIM ** -0.5
    scale[:, 3 * DIFF_WIDTH:3 * DIFF_WIDTH + MOBA_WIDTH] = LOG2_E * MOBA_HEAD_DIM ** -0.5
    return jnp.asarray(scale)


def kernel(x, g_ffn1, w_ffn1_gu, w_ffn1_down, g_mix, w_in, lam_q1, lam_k1, lam_q2, lam_k2, g_subln, p_a, p_b, w_o, g_ffn2, w_ffn2_gu, w_ffn2_down, g_final):
    b, s, d = x.shape
    assert g_ffn1.shape[0] == 1, "single-layer stack"
    xt = x.reshape(b * s, d)
    g_final_row = g_final.reshape(1, d)

    x1 = _ffn(xt, g_ffn1, w_ffn1_gu[0], w_ffn1_down[0], g_final_row, final_norm=False)

    qkv = _qkv_proj(x1, g_mix, w_in[0], _qkv_col_scale()).reshape(b, s, QKV_COLS)
    o_a = _diff_attn(qkv, _alibi_slopes(N_HEADS_DIFF), lam_q1, lam_k1, lam_q2, lam_k2, g_subln)
    o_b, w_gates, (p_a_bf, p_b_bf, w_o_bf) = _moba_attn(
        qkv, _alibi_slopes(N_HEADS_MOBA), w_in[0], (p_a[0], p_b[0], w_o[0]))
    x2 = _mix_out(x1, g_mix, o_a.reshape(b * s, DIFF_WIDTH), o_b.reshape(b * s, MOBA_WIDTH),
                  w_gates, p_a_bf, p_b_bf, w_o_bf)

    out = _ffn(x2, g_ffn2, w_ffn2_gu[0], w_ffn2_down[0], g_final_row, final_norm=True)
    return out.reshape(b, s, d)
```

```python
import functools
import itertools
import math

import jax
import jax.numpy as jnp
import numpy as np
from jax import lax
from jax.experimental import pallas as pl
from jax.experimental.pallas import tpu as pltpu

D_MODEL = 2048
N_HEADS_DIFF = 8
DIFF_HEAD_DIM = 64
DIFF_WIDTH = N_HEADS_DIFF * 2 * DIFF_HEAD_DIM
N_HEADS_MOBA = 8
MOBA_HEAD_DIM = 128
MOBA_WIDTH = N_HEADS_MOBA * MOBA_HEAD_DIM
MOBA_BLOCK = 256
MOBA_TOPK = 3
D_FF = 5632
RMS_EPS = 1e-6
QKV_COLS = 3 * DIFF_WIDTH + 3 * MOBA_WIDTH
LAM_INIT = 0.8 - 0.6 * math.exp(-0.3 * 0)

LANES = 128
SUBLANES = 8
VMEM_LIMIT_BYTES = 63 * 1024 * 1024
MASK_VALUE = -1e30

FFN_TOKEN_TILE = 1024
FFN_FF_TILE = 512
PROJ_TOKEN_TILE = 2048
PROJ_COL_TILE = 1024
MIX_TOKEN_TILE = 1024
MIX_COL_TILE = 512
ATTN_TILE = 256

_NT_DIMS = (((1,), (1,)), ((), ()))


def _rms_normalize(x, gain):
    ms = jnp.mean(x * x, axis=-1, keepdims=True)
    return x * lax.rsqrt(ms + RMS_EPS) * gain


def _compiler_params(semantics):
    return pltpu.CompilerParams(dimension_semantics=semantics, vmem_limit_bytes=VMEM_LIMIT_BYTES)


def _on_token_tile(x_hbm, x_buf, x_sem, consume):
    i = pl.program_id(0)
    j = pl.program_id(1)
    tm = x_buf.shape[0]

    def copy(tile):
        return pltpu.make_async_copy(x_hbm.at[pl.ds(tile * tm, tm), :], x_buf, x_sem)

    @pl.when((i == 0) & (j == 0))
    def _():
        copy(0).start()

    @pl.when(j == 0)
    def _():
        copy(i).wait()
        consume()

    @pl.when((j == 1) & (i + 1 < pl.num_programs(0)))
    def _():
        copy(i + 1).start()


_X_IN_HBM = pl.BlockSpec(memory_space=pl.ANY)
_SEQUENTIAL_GRID = ("arbitrary", "arbitrary")


def _token_tile_scratch(tm, d):
    return [pltpu.VMEM((tm, d), jnp.float32), pltpu.SemaphoreType.DMA(())]


def _ffn_kernel(x_hbm, g_ref, wg_ref, wu_ref, wd_ref, gf_ref, o_ref, h_ref, x_buf, x_sem,
                *, n_ff_tiles, final_norm):
    f = pl.program_id(1)
    bf16 = jnp.bfloat16
    last = n_ff_tiles - 1

    def half_swiglu():
        h = h_ref[...]
        gate = jnp.dot(h, wg_ref[...].astype(bf16), preferred_element_type=jnp.float32)
        up = jnp.dot(h, wu_ref[...].astype(bf16), preferred_element_type=jnp.float32)
        act = (gate * jax.nn.sigmoid(gate) * up * 0.5).astype(bf16)
        return jnp.dot(act, wd_ref[...].astype(bf16), preferred_element_type=jnp.float32)

    def first_step():
        x = x_buf[...]
        h_ref[...] = _rms_normalize(x, g_ref[...]).astype(bf16)
        o_ref[...] = x + half_swiglu()

    _on_token_tile(x_hbm, x_buf, x_sem, first_step)

    @pl.when((f > 0) & (f < last) if final_norm else (f > 0))
    def _():
        o_ref[...] += half_swiglu()

    if final_norm:
        @pl.when(f == last)
        def _():
            o_ref[...] = _rms_normalize(o_ref[...] + half_swiglu(), gf_ref[...])


def _ffn(x, gain, w_gu, w_down, final_gain, *, final_norm):
    t, d = x.shape
    d_ff = w_down.shape[0]
    tm, tf = FFN_TOKEN_TILE, FFN_FF_TILE
    n_ff_tiles = d_ff // tf
    assert t % tm == 0 and d_ff % tf == 0 and n_ff_tiles >= 2
    kern = functools.partial(_ffn_kernel, n_ff_tiles=n_ff_tiles, final_norm=final_norm)
    return pl.pallas_call(
        kern,
        grid=(t // tm, n_ff_tiles),
        in_specs=[
            _X_IN_HBM,
            pl.BlockSpec((1, d), lambda i, f: (0, 0)),
            pl.BlockSpec((d, tf), lambda i, f: (0, f)),
            pl.BlockSpec((d, tf), lambda i, f: (0, f + n_ff_tiles)),
            pl.BlockSpec((tf, d), lambda i, f: (f, 0)),
            pl.BlockSpec((1, d), lambda i, f: (0, 0)),
        ],
        out_specs=pl.BlockSpec((tm, d), lambda i, f: (i, 0)),
        out_shape=jax.ShapeDtypeStruct((t, d), jnp.float32),
        scratch_shapes=[pltpu.VMEM((tm, d), jnp.bfloat16)] + _token_tile_scratch(tm, d),
        compiler_params=_compiler_params(_SEQUENTIAL_GRID),
        name="ffn_final" if final_norm else "ffn",
    )(x, gain, w_gu, w_gu, w_down, final_gain)


def _qkv_proj_kernel(x_hbm, g_ref, w_ref, s_ref, o_ref, h_ref, x_buf, x_sem):
    def start_tile():
        h_ref[...] = _rms_normalize(x_buf[...], g_ref[...]).astype(jnp.bfloat16)

    _on_token_tile(x_hbm, x_buf, x_sem, start_tile)

    acc = jnp.dot(h_ref[...], w_ref[...].astype(jnp.bfloat16), preferred_element_type=jnp.float32)
    o_ref[...] = (acc * s_ref[...]).astype(o_ref.dtype)


def _qkv_proj(x, gain, w_in, col_scale):
    t, d = x.shape
    tm, tn = PROJ_TOKEN_TILE, PROJ_COL_TILE
    assert t % tm == 0 and QKV_COLS % tn == 0 and QKV_COLS // tn >= 2
    return pl.pallas_call(
        _qkv_proj_kernel,
        grid=(t // tm, QKV_COLS // tn),
        in_specs=[
            _X_IN_HBM,
            pl.BlockSpec((1, d), lambda i, j: (0, 0)),
            pl.BlockSpec((d, tn), lambda i, j: (0, j)),
            pl.BlockSpec((1, tn), lambda i, j: (0, j)),
        ],
        out_specs=pl.BlockSpec((tm, tn), lambda i, j: (i, j)),
        out_shape=jax.ShapeDtypeStruct((t, QKV_COLS), jnp.bfloat16),
        scratch_shapes=[pltpu.VMEM((tm, d), jnp.bfloat16)] + _token_tile_scratch(tm, d),
        compiler_params=_compiler_params(_SEQUENTIAL_GRID),
        name="qkv_proj",
    )(x, gain, w_in, col_scale)


ALIBI_TERMS = 3


def _alibi_columns(slope, t):
    lane = lax.broadcasted_iota(jnp.int32, (t, LANES), 1)
    rest = lax.broadcasted_iota(jnp.int32, (t, LANES), 0).astype(jnp.float32) * slope
    k_extra = jnp.zeros((t, LANES), jnp.float32)
    for i in range(ALIBI_TERMS):
        term = rest.astype(jnp.bfloat16).astype(jnp.float32)
        k_extra = jnp.where(lane == i, term, k_extra)
        rest = rest - term
    q_extra = jnp.where(lane < ALIBI_TERMS, 1.0, 0.0)
    return k_extra.astype(jnp.bfloat16), q_extra.astype(jnp.bfloat16)


def _query_shift(slope, t):
    return lax.broadcasted_iota(jnp.int32, (1, t), 1).astype(jnp.float32) * (-slope)


def _causal_mask_t(t):
    kk = lax.broadcasted_iota(jnp.int32, (t, t), 0)
    qq = lax.broadcasted_iota(jnp.int32, (t, t), 1)
    return jnp.where(qq >= kk, 0.0, MASK_VALUE)


def _fold_rows(x, op):
    rows, cols = x.shape
    return op(x.reshape(rows // SUBLANES, SUBLANES, cols), axis=0)


def _score_tiles(k_tile, q, shifts, mask_diag, s_ref, result):
    n, t = len(shifts), q.shape[0]
    cand = None
    for j in range(n):
        sj = lax.dot_general(k_tile(j), q, _NT_DIMS, preferred_element_type=jnp.float32)
        if j == n - 1:
            sj = sj + mask_diag
        s_ref[j * t:(j + 1) * t, :] = sj
        cj = _fold_rows(sj, jnp.max) + shifts[j]
        cand = cj if cand is None else jnp.maximum(cand, cj)
        yield
    result.append(jnp.max(cand, axis=0, keepdims=True))


def _softmax_pv(s_ref, m, shifts, vt_tile, result):
    t = m.shape[1]
    lpart, acc = None, None
    for j in range(len(shifts)):
        sj = s_ref[j * t:(j + 1) * t, :]
        p = jnp.exp2(sj - (m - shifts[j]))
        pj = _fold_rows(p, jnp.sum)
        lpart = pj if lpart is None else lpart + pj
        vt = vt_tile(j)
        pv = jnp.dot(vt, p.astype(vt.dtype), preferred_element_type=jnp.float32)
        acc = pv if acc is None else acc + pv
        yield
    result.append(acc / jnp.sum(lpart, axis=0, keepdims=True))


def _alternate(*steps):
    for _ in itertools.zip_longest(*steps):
        yield


def _run_pipelined(stage_a, stage_b, n, slots):
    ahead = slots - 1
    states = {}

    def run_a(u):
        if u < n:
            steps, states[u] = stage_a(u)
            for _ in steps:
                pass

    for u in range(ahead):
        run_a(u)
    for u in range(n):
        run_a(u + ahead)
        for _ in stage_b(u, states.pop(u)):
            pass


ATTN_HEADS_PER_STEP = 2
ATTN_SCORE_SLOTS = 3


def _diff_attn_kernel(slopes_ref, lq1_ref, lk1_ref, lq2_ref, lk2_ref, gs_ref,
                      q_ref, k_ref, v_ref, o_ref, vt_ref, s_ref):
    t, hw, nh, slots = ATTN_TILE, 2 * DIFF_HEAD_DIM, ATTN_HEADS_PER_STEP, s_ref.shape[0]
    n_tiles = q_ref.shape[1] // t
    mask_diag = _causal_mask_t(t)
    lam = (jnp.exp(jnp.sum(lq1_ref[...] * lk1_ref[...], axis=-1, keepdims=True))
           - jnp.exp(jnp.sum(lq2_ref[...] * lk2_ref[...], axis=-1, keepdims=True))
           + LAM_INIT)
    lane = lax.broadcasted_iota(jnp.int32, (t, LANES), 1)

    def head_setup(hh):
        cols = slice(hh * hw, (hh + 1) * hw)
        slope = slopes_ref[pl.program_id(1) * nh + hh]
        vt_ref[hh] = v_ref[0, :, cols].T
        k_extra, q_extra = _alibi_columns(slope, t)
        q_shift = _query_shift(slope, t)
        return dict(
            cols=cols, q_extra=q_extra,
            shifts=lambda c: [q_shift - slope * float((c - j) * t) for j in range(c + 1)],
            k_tile=lambda j: jnp.concatenate([k_ref[0, j * t:(j + 1) * t, cols], k_extra], axis=1),
            vt_tile=lambda j: vt_ref[hh, :, j * t:(j + 1) * t])

    heads = [head_setup(hh) for hh in range(nh)]
    streams = [(hh, mi) for hh in range(nh) for mi in range(2)]

    def scores(c):
        maxes, steps = [], []
        for hh, mi in streams:
            hd = heads[hh]
            q = q_ref[0, c * t:(c + 1) * t, hd["cols"]]
            keep = (lane < DIFF_HEAD_DIM) if mi == 0 else (lane >= DIFF_HEAD_DIM)
            qm = jnp.where(keep, q, jnp.zeros_like(q))
            maxes.append([])
            steps.append(_score_tiles(hd["k_tile"], jnp.concatenate([qm, hd["q_extra"]], axis=1),
                                      hd["shifts"](c), mask_diag,
                                      s_ref.at[c % slots, 2 * hh + mi], maxes[-1]))
        return _alternate(*steps), maxes

    def finish(c, maxes):
        outs = [[] for _ in streams]
        yield from _alternate(*[
            _softmax_pv(s_ref.at[c % slots, 2 * hh + mi], maxes[si][0], heads[hh]["shifts"](c),
                        heads[hh]["vt_tile"], outs[si])
            for si, (hh, mi) in enumerate(streams)])
        for hh, hd in enumerate(heads):
            o = (outs[2 * hh][0] - lam * outs[2 * hh + 1][0]).T
            o_ref[0, c * t:(c + 1) * t, hd["cols"]] = (
                _rms_normalize(o, gs_ref[...]) * (1.0 - LAM_INIT)).astype(o_ref.dtype)

    _run_pipelined(scores, finish, n_tiles, slots)


def _diff_attn(qkv, slopes, lam_q1, lam_k1, lam_q2, lam_k2, g_subln):
    b, s, _ = qkv.shape
    nh, hw = N_HEADS_DIFF, 2 * DIFF_HEAD_DIM
    hps = ATTN_HEADS_PER_STEP
    w = hps * hw
    assert hw == LANES and s % ATTN_TILE == 0 and nh % hps == 0
    k_blk0, v_blk0 = DIFF_WIDTH // w, 2 * DIFF_WIDTH // w
    lam_spec = pl.BlockSpec((1, DIFF_HEAD_DIM), lambda bi, g: (0, 0))
    return pl.pallas_call(
        _diff_attn_kernel,
        grid=(b, nh // hps),
        in_specs=[
            pl.BlockSpec(memory_space=pltpu.SMEM),
            lam_spec, lam_spec, lam_spec, lam_spec,
            pl.BlockSpec((1, hw), lambda bi, g: (0, 0)),
            pl.BlockSpec((1, s, w), lambda bi, g: (bi, 0, g)),
            pl.BlockSpec((1, s, w), lambda bi, g: (bi, 0, k_blk0 + g)),
            pl.BlockSpec((1, s, w), lambda bi, g: (bi, 0, v_blk0 + g)),
        ],
        out_specs=pl.BlockSpec((1, s, w), lambda bi, g: (bi, 0, g)),
        out_shape=jax.ShapeDtypeStruct((b, s, DIFF_WIDTH), jnp.bfloat16),
        scratch_shapes=[pltpu.VMEM((hps, hw, s), jnp.bfloat16),
                        pltpu.VMEM((ATTN_SCORE_SLOTS, 2 * hps, s, ATTN_TILE), jnp.float32)],
        compiler_params=_compiler_params(("parallel", "parallel")),
        name="diff_attn",
    )(slopes, lam_q1, lam_k1, lam_q2, lam_k2, g_subln, qkv, qkv, qkv)


MOBA_HEADS_PER_STEP = 2
MOBA_SCORE_SLOTS = 3


def _moba_attn_kernel(slopes_ref, q_ref, k_ref, v_ref, *refs, n_blocks, n_sel, n_casts):
    cast_src, o_ref, cast_dst = refs[:n_casts], refs[n_casts], refs[n_casts + 1:2 * n_casts + 1]
    vt_ref, s_ref = refs[2 * n_casts + 1:]
    for src, dst in zip(cast_src, cast_dst):
        dst[...] = src[...].astype(dst.dtype)

    blk, dh, nh, slots = MOBA_BLOCK, MOBA_HEAD_DIM, MOBA_HEADS_PER_STEP, s_ref.shape[0]
    mask_diag = _causal_mask_t(blk)
    blk_id = lax.broadcasted_iota(jnp.int32, (n_blocks, blk), 0)

    def head_setup(hh):
        cols = slice(hh * dh, (hh + 1) * dh)
        slope = slopes_ref[pl.program_id(1) * nh + hh]
        vt_ref[hh] = v_ref[0, :, cols].T
        k_extra, q_extra = _alibi_columns(slope, blk)
        kmean = jnp.mean(k_ref[0, :, cols].astype(jnp.float32).reshape(n_blocks, blk, dh), axis=1)
        kmean = jnp.concatenate([kmean, jnp.zeros((LANES - n_blocks, dh), jnp.float32)], axis=0)
        kmean_hi = kmean.astype(jnp.bfloat16)
        return dict(
            cols=cols, slope=slope, q_extra=q_extra, q_shift=_query_shift(slope, blk),
            kmean_hi=kmean_hi, kmean_lo=(kmean - kmean_hi.astype(jnp.float32)).astype(jnp.bfloat16),
            k_tile=lambda j: jnp.concatenate([k_ref[0, j * blk:(j + 1) * blk, cols], k_extra], axis=1),
            vt_tile=lambda j: vt_ref[hh, :, j * blk:(j + 1) * blk])

    heads = [head_setup(hh) for hh in range(nh)]

    def head_scores(c, hh, hd):
        q = q_ref[0, c * blk:(c + 1) * blk, hd["cols"]]
        g = (lax.dot_general(hd["kmean_hi"], q, _NT_DIMS, preferred_element_type=jnp.float32)
             + lax.dot_general(hd["kmean_lo"], q, _NT_DIMS, preferred_element_type=jnp.float32))[:n_blocks]
        past = blk_id < c
        g = jnp.where(past, g, -jnp.inf)
        rank = jnp.zeros(g.shape, jnp.int32)
        for r in range(1, n_blocks):
            other = pltpu.roll(g, r, 0)
            rank += jnp.where(blk_id >= r, (other >= g).astype(jnp.int32), (other > g).astype(jnp.int32))
        keep = (past & (rank < n_sel)) | (blk_id == c)
        sel_bias = jnp.where(keep, 0.0, MASK_VALUE)
        shifts = [sel_bias[j:j + 1, :] + (hd["q_shift"] - hd["slope"] * float((c - j) * blk))
                  for j in range(c + 1)]
        q_aug = jnp.concatenate([q, hd["q_extra"]], axis=1)
        m = []
        steps = _score_tiles(hd["k_tile"], q_aug, shifts, mask_diag, s_ref.at[c % slots, hh], m)
        return steps, (m, shifts)

    def scores(c):
        per_head = [head_scores(c, hh, hd) for hh, hd in enumerate(heads)]
        return _alternate(*[steps for steps, _ in per_head]), [state for _, state in per_head]

    def finish(c, states):
        outs = [[] for _ in heads]
        yield from _alternate(*[
            _softmax_pv(s_ref.at[c % slots, hh], m[0], shifts, heads[hh]["vt_tile"], outs[hh])
            for hh, (m, shifts) in enumerate(states)])
        for hh, hd in enumerate(heads):
            o_ref[0, c * blk:(c + 1) * blk, hd["cols"]] = outs[hh][0].T.astype(o_ref.dtype)

    _run_pipelined(scores, finish, n_blocks, slots)


def _moba_attn(qkv, slopes, w_in, row_weights):
    b, s, _ = qkv.shape
    nh, dh, blk = N_HEADS_MOBA, MOBA_HEAD_DIM, MOBA_BLOCK
    assert dh == LANES and s % blk == 0
    n_blocks = s // blk
    assert n_blocks == SUBLANES, "block ranking uses one vreg row per MoBA block"
    n_sel = min(MOBA_TOPK, n_blocks - 1)
    hps = MOBA_HEADS_PER_STEP
    w = hps * dh
    assert nh % hps == 0 and (3 * DIFF_WIDTH) % w == 0 and MOBA_WIDTH % w == 0
    q_blk0 = 3 * DIFF_WIDTH // w
    k_blk0, v_blk0 = q_blk0 + nh // hps, q_blk0 + 2 * nh // hps

    n_g = nh // hps
    n_steps = b * n_g
    step = lambda bi, g: bi * n_g + g
    d, gate_cols = w_in.shape[0], w_in.shape[1] - QKV_COLS
    gc = gate_cols // n_steps
    assert gate_cols % n_steps == 0 and gc % LANES == 0 and QKV_COLS % gc == 0
    cast_in = [pl.BlockSpec((d, gc), lambda bi, g: (0, QKV_COLS // gc + step(bi, g)))]
    cast_out = [pl.BlockSpec((d, gc), lambda bi, g: (0, step(bi, g)))]
    cast_shapes = [jax.ShapeDtypeStruct((d, gate_cols), jnp.bfloat16)]
    for m in row_weights:
        rows = m.shape[0] // n_steps
        assert m.shape[0] % n_steps == 0 and rows % (2 * SUBLANES) == 0
        spec = pl.BlockSpec((rows, m.shape[1]), lambda bi, g: (step(bi, g), 0))
        cast_in.append(spec)
        cast_out.append(spec)
        cast_shapes.append(jax.ShapeDtypeStruct(m.shape, jnp.bfloat16))

    kern = functools.partial(_moba_attn_kernel, n_blocks=n_blocks, n_sel=n_sel, n_casts=len(cast_in))
    o_b, w_gates, *row_weights_bf = pl.pallas_call(
        kern,
        grid=(b, n_g),
        in_specs=[
            pl.BlockSpec(memory_space=pltpu.SMEM),
            pl.BlockSpec((1, s, w), lambda bi, g: (bi, 0, q_blk0 + g)),
            pl.BlockSpec((1, s, w), lambda bi, g: (bi, 0, k_blk0 + g)),
            pl.BlockSpec((1, s, w), lambda bi, g: (bi, 0, v_blk0 + g)),
        ] + cast_in,
        out_specs=[pl.BlockSpec((1, s, w), lambda bi, g: (bi, 0, g))] + cast_out,
        out_shape=[jax.ShapeDtypeStruct((b, s, MOBA_WIDTH), jnp.bfloat16)] + cast_shapes,
        scratch_shapes=[pltpu.VMEM((hps, dh, s), jnp.bfloat16),
                        pltpu.VMEM((MOBA_SCORE_SLOTS, hps, s, blk), jnp.float32)],
        compiler_params=_compiler_params(("parallel", "parallel")),
        name="moba_attn",
    )(slopes, qkv, qkv, qkv, w_in, *row_weights)
    return o_b, w_gates, row_weights_bf


MIXERS_SCORE_SLOTS = 2


def _mixers_kernel(slopes_a_ref, slopes_b_ref, lq1_ref, lk1_ref, lq2_ref, lk2_ref, gs_ref,
                   qa_ref, ka_ref, va_ref, qb_ref, kb_ref, vb_ref, *refs, n_blocks, n_sel, n_casts):
    cast_src = refs[:n_casts]
    oa_ref, ob_ref = refs[n_casts], refs[n_casts + 1]
    cast_dst = refs[n_casts + 2:2 * n_casts + 2]
    vta_ref, sa_ref, vtb_ref, sb_ref = refs[2 * n_casts + 2:]
    _diff_attn_kernel(slopes_a_ref, lq1_ref, lk1_ref, lq2_ref, lk2_ref, gs_ref,
                      qa_ref, ka_ref, va_ref, oa_ref, vta_ref, sa_ref)
    _moba_attn_kernel(slopes_b_ref, qb_ref, kb_ref, vb_ref, *cast_src, ob_ref, *cast_dst, vtb_ref, sb_ref,
                      n_blocks=n_blocks, n_sel=n_sel, n_casts=n_casts)


def _mixers(qkv, slopes_a, slopes_b, lam_q1, lam_k1, lam_q2, lam_k2, g_subln, w_in, row_weights):
    b, s, _ = qkv.shape
    hps = ATTN_HEADS_PER_STEP
    assert hps == MOBA_HEADS_PER_STEP and N_HEADS_DIFF == N_HEADS_MOBA
    nh, hw, dh, blk = N_HEADS_DIFF, 2 * DIFF_HEAD_DIM, MOBA_HEAD_DIM, MOBA_BLOCK
    assert hw == LANES and dh == LANES and s % ATTN_TILE == 0 and s % blk == 0 and nh % hps == 0
    n_blocks = s // blk
    assert n_blocks == SUBLANES, "block ranking uses one vreg row per MoBA block"
    n_sel = min(MOBA_TOPK, n_blocks - 1)
    w = hps * LANES
    n_g = nh // hps
    ka0, va0 = DIFF_WIDTH // w, 2 * DIFF_WIDTH // w
    qb0 = 3 * DIFF_WIDTH // w
    kb0, vb0 = qb0 + n_g, qb0 + 2 * n_g
    head_block = lambda first: pl.BlockSpec((1, s, w), lambda bi, g: (bi, 0, first + g))
    lam_spec = pl.BlockSpec((1, DIFF_HEAD_DIM), lambda bi, g: (0, 0))

    n_steps = b * n_g
    step = lambda bi, g: bi * n_g + g
    d, gate_cols = w_in.shape[0], w_in.shape[1] - QKV_COLS
    gc = gate_cols // n_steps
    assert gate_cols % n_steps == 0 and gc % LANES == 0 and QKV_COLS % gc == 0
    cast_in = [pl.BlockSpec((d, gc), lambda bi, g: (0, QKV_COLS // gc + step(bi, g)))]
    cast_out = [pl.BlockSpec((d, gc), lambda bi, g: (0, step(bi, g)))]
    cast_shapes = [jax.ShapeDtypeStruct((d, gate_cols), jnp.bfloat16)]
    for m in row_weights:
        rows = m.shape[0] // n_steps
        assert m.shape[0] % n_steps == 0 and rows % (2 * SUBLANES) == 0
        spec = pl.BlockSpec((rows, m.shape[1]), lambda bi, g: (step(bi, g), 0))
        cast_in.append(spec)
        cast_out.append(spec)
        cast_shapes.append(jax.ShapeDtypeStruct(m.shape, jnp.bfloat16))

    kern = functools.partial(_mixers_kernel, n_blocks=n_blocks, n_sel=n_sel, n_casts=len(cast_in))
    o_a, o_b, w_gates, *row_weights_bf = pl.pallas_call(
        kern,
        grid=(b, n_g),
        in_specs=[
            pl.BlockSpec(memory_space=pltpu.SMEM), pl.BlockSpec(memory_space=pltpu.SMEM),
            lam_spec, lam_spec, lam_spec, lam_spec,
            pl.BlockSpec((1, hw), lambda bi, g: (0, 0)),
            head_block(0), head_block(ka0), head_block(va0),
            head_block(qb0), head_block(kb0), head_block(vb0),
        ] + cast_in,
        out_specs=[head_block(0), head_block(0)] + cast_out,
        out_shape=[jax.ShapeDtypeStruct((b, s, DIFF_WIDTH), jnp.bfloat16),
                   jax.ShapeDtypeStruct((b, s, MOBA_WIDTH), jnp.bfloat16)] + cast_shapes,
        scratch_shapes=[pltpu.VMEM((hps, hw, s), jnp.bfloat16),
                        pltpu.VMEM((MIXERS_SCORE_SLOTS, 2 * hps, s, ATTN_TILE), jnp.float32),
                        pltpu.VMEM((hps, dh, s), jnp.bfloat16),
                        pltpu.VMEM((MIXERS_SCORE_SLOTS, hps, s, blk), jnp.float32)],
        compiler_params=_compiler_params(("parallel", "parallel")),
        name="mixers",
    )(slopes_a, slopes_b, lam_q1, lam_k1, lam_q2, lam_k2, g_subln, qkv, qkv, qkv, qkv, qkv, qkv,
      w_in, *row_weights)
    return o_a, o_b, w_gates, row_weights_bf


def _mix_out_kernel(x_hbm, g_ref, oa_ref, ob_ref, wga_ref, wgb_ref, pa_ref, pb_ref, wo_ref, o_ref,
                    h_ref, x_buf, x_sem):
    def start_tile():
        x = x_buf[...]
        h_ref[...] = _rms_normalize(x, g_ref[...]).astype(jnp.bfloat16)
        o_ref[...] = x

    _on_token_tile(x_hbm, x_buf, x_sem, start_tile)

    h = h_ref[...]
    gate_a = jnp.dot(h, wga_ref[...], preferred_element_type=jnp.float32)
    gate_b = jnp.dot(h, wgb_ref[...], preferred_element_type=jnp.float32)
    proj_a = jnp.dot(oa_ref[...], pa_ref[...], preferred_element_type=jnp.float32)
    proj_b = jnp.dot(ob_ref[...], pb_ref[...], preferred_element_type=jnp.float32)
    merged = jax.nn.sigmoid(gate_a) * proj_a + jax.nn.sigmoid(gate_b) * proj_b
    o_ref[...] += jnp.dot(merged.astype(jnp.bfloat16), wo_ref[...], preferred_element_type=jnp.float32)


def _mix_out(x, gain, o_a, o_b, w_gates, p_a, p_b, w_o):
    t, d = x.shape
    tm, tc = MIX_TOKEN_TILE, MIX_COL_TILE
    assert t % tm == 0 and d % tc == 0 and d // tc >= 2
    ga_blk0, gb_blk0 = 0, d // tc
    return pl.pallas_call(
        _mix_out_kernel,
        grid=(t // tm, d // tc),
        in_specs=[
            _X_IN_HBM,
            pl.BlockSpec((1, d), lambda i, c: (0, 0)),
            pl.BlockSpec((tm, DIFF_WIDTH), lambda i, c: (i, 0)),
            pl.BlockSpec((tm, MOBA_WIDTH), lambda i, c: (i, 0)),
            pl.BlockSpec((d, tc), lambda i, c: (0, ga_blk0 + c)),
            pl.BlockSpec((d, tc), lambda i, c: (0, gb_blk0 + c)),
            pl.BlockSpec((DIFF_WIDTH, tc), lambda i, c: (0, c)),
            pl.BlockSpec((MOBA_WIDTH, tc), lambda i, c: (0, c)),
            pl.BlockSpec((tc, d), lambda i, c: (c, 0)),
        ],
        out_specs=pl.BlockSpec((tm, d), lambda i, c: (i, 0)),
        out_shape=jax.ShapeDtypeStruct((t, d), jnp.float32),
        scratch_shapes=[pltpu.VMEM((tm, d), jnp.bfloat16)] + _token_tile_scratch(tm, d),
        compiler_params=_compiler_params(_SEQUENTIAL_GRID),
        name="mix_out",
    )(x, gain, o_a, o_b, w_gates, w_gates, p_a, p_b, w_o)


LOG2_E = math.log2(math.e)


def _alibi_slopes(n):
    return jnp.asarray(LOG2_E * 2.0 ** (-8.0 * np.arange(1, n + 1) / n), dtype=jnp.float32)


def _qkv_col_scale():
    scale = np.ones((1, QKV_COLS), np.float32)
    scale[:, :DIFF_WIDTH] = LOG2_E * DIFF_HEAD_DIM ** -0.5
    scale[:, 3 * DIFF_WIDTH:3 * DIFF_WIDTH + MOBA_WIDTH] = LOG2_E * MOBA_HEAD_DIM ** -0.5
    return jnp.asarray(scale)


def kernel(x, g_ffn1, w_ffn1_gu, w_ffn1_down, g_mix, w_in, lam_q1, lam_k1, lam_q2, lam_k2, g_subln, p_a, p_b, w_o, g_ffn2, w_ffn2_gu, w_ffn2_down, g_final):
    b, s, d = x.shape
    assert g_ffn1.shape[0] == 1, "single-layer stack"
    xt = x.reshape(b * s, d)
    g_final_row = g_final.reshape(1, d)

    x1 = _ffn(xt, g_ffn1, w_ffn1_gu[0], w_ffn1_down[0], g_final_row, final_norm=False)

    qkv = _qkv_proj(x1, g_mix, w_in[0], _qkv_col_scale()).reshape(b, s, QKV_COLS)
    o_a, o_b, w_gates, (p_a_bf, p_b_bf, w_o_bf) = _mixers(
        qkv, _alibi_slopes(N_HEADS_DIFF), _alibi_slopes(N_HEADS_MOBA), lam_q1, lam_k1, lam_q2, lam_k2,
        g_subln, w_in[0], (p_a[0], p_b[0], w_o[0]))
    x2 = _mix_out(x1, g_mix, o_a.reshape(b * s, DIFF_WIDTH), o_b.reshape(b * s, MOBA_WIDTH),
                  w_gates, p_a_bf, p_b_bf, w_o_bf)

    out = _ffn(x2, g_ffn2, w_ffn2_gu[0], w_ffn2_down[0], g_final_row, final_norm=True)
    return out.reshape(b, s, d)
```

```python
import functools
import itertools
import math

import jax
import jax.numpy as jnp
import numpy as np
from jax import lax
from jax.experimental import pallas as pl
from jax.experimental.pallas import tpu as pltpu

D_MODEL = 2048
N_HEADS_DIFF = 8
DIFF_HEAD_DIM = 64
DIFF_WIDTH = N_HEADS_DIFF * 2 * DIFF_HEAD_DIM
N_HEADS_MOBA = 8
MOBA_HEAD_DIM = 128
MOBA_WIDTH = N_HEADS_MOBA * MOBA_HEAD_DIM
MOBA_BLOCK = 256
MOBA_TOPK = 3
D_FF = 5632
RMS_EPS = 1e-6
QKV_COLS = 3 * DIFF_WIDTH + 3 * MOBA_WIDTH
LAM_INIT = 0.8 - 0.6 * math.exp(-0.3 * 0)

LANES = 128
SUBLANES = 8
VMEM_LIMIT_BYTES = 63 * 1024 * 1024
MASK_VALUE = -1e30

FFN_TOKEN_TILE = 1024
FFN_FF_TILE = 512
PROJ_TOKEN_TILE = 2048
PROJ_COL_TILE = 1024
MIX_TOKEN_TILE = 1024
MIX_COL_TILE = 512
ATTN_TILE = 256

_NT_DIMS = (((1,), (1,)), ((), ()))


def _rms_normalize(x, gain):
    ms = jnp.mean(x * x, axis=-1, keepdims=True)
    return x * lax.rsqrt(ms + RMS_EPS) * gain


def _compiler_params(semantics):
    return pltpu.CompilerParams(dimension_semantics=semantics, vmem_limit_bytes=VMEM_LIMIT_BYTES)


def _on_token_tile(x_hbm, x_buf, x_sem, consume):
    i = pl.program_id(0)
    j = pl.program_id(1)
    tm = x_buf.shape[0]

    def copy(tile):
        return pltpu.make_async_copy(x_hbm.at[pl.ds(tile * tm, tm), :], x_buf, x_sem)

    @pl.when((i == 0) & (j == 0))
    def _():
        copy(0).start()

    @pl.when(j == 0)
    def _():
        copy(i).wait()
        consume()

    @pl.when((j == 1) & (i + 1 < pl.num_programs(0)))
    def _():
        copy(i + 1).start()


_X_IN_HBM = pl.BlockSpec(memory_space=pl.ANY)
_SEQUENTIAL_GRID = ("arbitrary", "arbitrary")


def _token_tile_scratch(tm, d):
    return [pltpu.VMEM((tm, d), jnp.float32), pltpu.SemaphoreType.DMA(())]


def _ffn_kernel(x_hbm, g_ref, wg_ref, wu_ref, wd_ref, gf_ref, o_ref, h_ref, x_buf, x_sem,
                *, n_ff_tiles, final_norm):
    f = pl.program_id(1)
    bf16 = jnp.bfloat16
    last = n_ff_tiles - 1

    def half_swiglu():
        h = h_ref[...]
        gate = jnp.dot(h, wg_ref[...].astype(bf16), preferred_element_type=jnp.float32)
        up = jnp.dot(h, wu_ref[...].astype(bf16), preferred_element_type=jnp.float32)
        act = (gate * jax.nn.sigmoid(gate) * up * 0.5).astype(bf16)
        return jnp.dot(act, wd_ref[...].astype(bf16), preferred_element_type=jnp.float32)

    def first_step():
        x = x_buf[...]
        h_ref[...] = _rms_normalize(x, g_ref[...]).astype(bf16)
        o_ref[...] = x + half_swiglu()

    _on_token_tile(x_hbm, x_buf, x_sem, first_step)

    @pl.when((f > 0) & (f < last) if final_norm else (f > 0))
    def _():
        o_ref[...] += half_swiglu()

    if final_norm:
        @pl.when(f == last)
        def _():
            o_ref[...] = _rms_normalize(o_ref[...] + half_swiglu(), gf_ref[...])


def _ffn(x, gain, w_gu, w_down, final_gain, *, final_norm):
    t, d = x.shape
    d_ff = w_down.shape[0]
    tm, tf = FFN_TOKEN_TILE, FFN_FF_TILE
    n_ff_tiles = d_ff // tf
    assert t % tm == 0 and d_ff % tf == 0 and n_ff_tiles >= 2
    kern = functools.partial(_ffn_kernel, n_ff_tiles=n_ff_tiles, final_norm=final_norm)
    return pl.pallas_call(
        kern,
        grid=(t // tm, n_ff_tiles),
        in_specs=[
            _X_IN_HBM,
            pl.BlockSpec((1, d), lambda i, f: (0, 0)),
            pl.BlockSpec((d, tf), lambda i, f: (0, f)),
            pl.BlockSpec((d, tf), lambda i, f: (0, f + n_ff_tiles)),
            pl.BlockSpec((tf, d), lambda i, f: (f, 0)),
            pl.BlockSpec((1, d), lambda i, f: (0, 0)),
        ],
        out_specs=pl.BlockSpec((tm, d), lambda i, f: (i, 0)),
        out_shape=jax.ShapeDtypeStruct((t, d), jnp.float32),
        scratch_shapes=[pltpu.VMEM((tm, d), jnp.bfloat16)] + _token_tile_scratch(tm, d),
        compiler_params=_compiler_params(_SEQUENTIAL_GRID),
        name="ffn_final" if final_norm else "ffn",
    )(x, gain, w_gu, w_gu, w_down, final_gain)


FFN_OUT_TILE = 256


def _ffn_two_phase_kernel(x_hbm, xcol_ref, g_ref, wg_ref, wu_ref, wd_ref, o_ref, h_ref, act_ref,
                          x_buf, x_sem, *, n_ff_tiles):
    f = pl.program_id(1)
    bf16 = jnp.bfloat16
    tf = act_ref.shape[2]

    def start_tile():
        h_ref[...] = _rms_normalize(x_buf[...], g_ref[...]).astype(bf16)

    _on_token_tile(x_hbm, x_buf, x_sem, start_tile)

    @pl.when(f < n_ff_tiles)
    def _():
        h = h_ref[...]
        gate = jnp.dot(h, wg_ref[...].astype(bf16), preferred_element_type=jnp.float32)
        up = jnp.dot(h, wu_ref[...].astype(bf16), preferred_element_type=jnp.float32)
        act_ref[f] = (gate * jax.nn.sigmoid(gate) * up * 0.5).astype(bf16)

    @pl.when(f >= n_ff_tiles)
    def _():
        delta = None
        for j in range(n_ff_tiles):
            part = jnp.dot(act_ref[j], wd_ref[j * tf:(j + 1) * tf, :].astype(bf16),
                           preferred_element_type=jnp.float32)
            delta = part if delta is None else delta + part
        o_ref[...] = xcol_ref[...] + delta


def _ffn_two_phase(x, gain, w_gu, w_down):
    t, d = x.shape
    d_ff = w_down.shape[0]
    tm, tf, tn = FFN_TOKEN_TILE, FFN_FF_TILE, FFN_OUT_TILE
    n1, n2 = d_ff // tf, d // tn
    assert t % tm == 0 and d_ff % tf == 0 and d % tn == 0 and n1 >= 2
    ff_step = lambda f: jnp.minimum(f, n1 - 1)
    out_step = lambda f: jnp.maximum(f - n1, 0)
    return pl.pallas_call(
        functools.partial(_ffn_two_phase_kernel, n_ff_tiles=n1),
        grid=(t // tm, n1 + n2),
        in_specs=[
            _X_IN_HBM,
            pl.BlockSpec((tm, tn), lambda i, f: (i, out_step(f))),
            pl.BlockSpec((1, d), lambda i, f: (0, 0)),
            pl.BlockSpec((d, tf), lambda i, f: (0, ff_step(f))),
            pl.BlockSpec((d, tf), lambda i, f: (0, ff_step(f) + n1)),
            pl.BlockSpec((d_ff, tn), lambda i, f: (0, out_step(f))),
        ],
        out_specs=pl.BlockSpec((tm, tn), lambda i, f: (i, out_step(f))),
        out_shape=jax.ShapeDtypeStruct((t, d), jnp.float32),
        scratch_shapes=[pltpu.VMEM((tm, d), jnp.bfloat16),
                        pltpu.VMEM((n1, tm, tf), jnp.bfloat16)] + _token_tile_scratch(tm, d),
        compiler_params=_compiler_params(_SEQUENTIAL_GRID),
        name="ffn",
    )(x, x, gain, w_gu, w_gu, w_down)


def _qkv_proj_kernel(x_hbm, g_ref, w_ref, s_ref, o_ref, h_ref, x_buf, x_sem):
    def start_tile():
        h_ref[...] = _rms_normalize(x_buf[...], g_ref[...]).astype(jnp.bfloat16)

    _on_token_tile(x_hbm, x_buf, x_sem, start_tile)

    acc = jnp.dot(h_ref[...], w_ref[...].astype(jnp.bfloat16), preferred_element_type=jnp.float32)
    o_ref[...] = (acc * s_ref[...]).astype(o_ref.dtype)


def _qkv_proj(x, gain, w_in, col_scale):
    t, d = x.shape
    tm, tn = PROJ_TOKEN_TILE, PROJ_COL_TILE
    assert t % tm == 0 and QKV_COLS % tn == 0 and QKV_COLS // tn >= 2
    return pl.pallas_call(
        _qkv_proj_kernel,
        grid=(t // tm, QKV_COLS // tn),
        in_specs=[
            _X_IN_HBM,
            pl.BlockSpec((1, d), lambda i, j: (0, 0)),
            pl.BlockSpec((d, tn), lambda i, j: (0, j)),
            pl.BlockSpec((1, tn), lambda i, j: (0, j)),
        ],
        out_specs=pl.BlockSpec((tm, tn), lambda i, j: (i, j)),
        out_shape=jax.ShapeDtypeStruct((t, QKV_COLS), jnp.bfloat16),
        scratch_shapes=[pltpu.VMEM((tm, d), jnp.bfloat16)] + _token_tile_scratch(tm, d),
        compiler_params=_compiler_params(_SEQUENTIAL_GRID),
        name="qkv_proj",
    )(x, gain, w_in, col_scale)


ALIBI_TERMS = 3


def _alibi_columns(slope, t):
    lane = lax.broadcasted_iota(jnp.int32, (t, LANES), 1)
    rest = lax.broadcasted_iota(jnp.int32, (t, LANES), 0).astype(jnp.float32) * slope
    k_extra = jnp.zeros((t, LANES), jnp.float32)
    for i in range(ALIBI_TERMS):
        term = rest.astype(jnp.bfloat16).astype(jnp.float32)
        k_extra = jnp.where(lane == i, term, k_extra)
        rest = rest - term
    q_extra = jnp.where(lane < ALIBI_TERMS, 1.0, 0.0)
    return k_extra.astype(jnp.bfloat16), q_extra.astype(jnp.bfloat16)


def _query_shift(slope, t):
    return lax.broadcasted_iota(jnp.int32, (1, t), 1).astype(jnp.float32) * (-slope)


def _causal_mask_t(t):
    kk = lax.broadcasted_iota(jnp.int32, (t, t), 0)
    qq = lax.broadcasted_iota(jnp.int32, (t, t), 1)
    return jnp.where(qq >= kk, 0.0, MASK_VALUE)


def _fold_rows(x, op):
    rows, cols = x.shape
    return op(x.reshape(rows // SUBLANES, SUBLANES, cols), axis=0)


def _score_tiles(k_tile, q, shifts, mask_diag, s_ref, result):
    n, t = len(shifts), q.shape[0]
    cand = None
    for j in range(n):
        sj = lax.dot_general(k_tile(j), q, _NT_DIMS, preferred_element_type=jnp.float32)
        if j == n - 1:
            sj = sj + mask_diag
        s_ref[j * t:(j + 1) * t, :] = sj
        cj = _fold_rows(sj, jnp.max) + shifts[j]
        cand = cj if cand is None else jnp.maximum(cand, cj)
        yield
    result.append(jnp.max(cand, axis=0, keepdims=True))


def _softmax_pv(s_ref, m, shifts, vt_tile, result):
    t = m.shape[1]
    lpart, acc = None, None
    for j in range(len(shifts)):
        sj = s_ref[j * t:(j + 1) * t, :]
        p = jnp.exp2(sj - (m - shifts[j]))
        pj = _fold_rows(p, jnp.sum)
        lpart = pj if lpart is None else lpart + pj
        vt = vt_tile(j)
        pv = jnp.dot(vt, p.astype(vt.dtype), preferred_element_type=jnp.float32)
        acc = pv if acc is None else acc + pv
        yield
    result.append(acc / jnp.sum(lpart, axis=0, keepdims=True))


def _alternate(*steps):
    for _ in itertools.zip_longest(*steps):
        yield


def _run_pipelined(stage_a, stage_b, n, slots):
    ahead = slots - 1
    states = {}

    def run_a(u):
        if u < n:
            steps, states[u] = stage_a(u)
            for _ in steps:
                pass

    for u in range(ahead):
        run_a(u)
    for u in range(n):
        run_a(u + ahead)
        for _ in stage_b(u, states.pop(u)):
            pass


ATTN_HEADS_PER_STEP = 2


def _diff_attn_kernel(slopes_ref, lq1_ref, lk1_ref, lq2_ref, lk2_ref, gs_ref,
                      q_ref, k_ref, v_ref, o_ref, vt_ref, s_ref):
    t, hw, nh, slots = ATTN_TILE, 2 * DIFF_HEAD_DIM, ATTN_HEADS_PER_STEP, s_ref.shape[0]
    n_tiles = q_ref.shape[1] // t
    mask_diag = _causal_mask_t(t)
    lam = (jnp.exp(jnp.sum(lq1_ref[...] * lk1_ref[...], axis=-1, keepdims=True))
           - jnp.exp(jnp.sum(lq2_ref[...] * lk2_ref[...], axis=-1, keepdims=True))
           + LAM_INIT)
    lane = lax.broadcasted_iota(jnp.int32, (t, LANES), 1)

    def head_setup(hh):
        cols = slice(hh * hw, (hh + 1) * hw)
        slope = slopes_ref[pl.program_id(1) * nh + hh]
        vt_ref[hh] = v_ref[0, :, cols].T
        k_extra, q_extra = _alibi_columns(slope, t)
        q_shift = _query_shift(slope, t)
        return dict(
            cols=cols, q_extra=q_extra,
            shifts=lambda c: [q_shift - slope * float((c - j) * t) for j in range(c + 1)],
            k_tile=lambda j: jnp.concatenate([k_ref[0, j * t:(j + 1) * t, cols], k_extra], axis=1),
            vt_tile=lambda j: vt_ref[hh, :, j * t:(j + 1) * t])

    heads = [head_setup(hh) for hh in range(nh)]
    streams = [(hh, mi) for hh in range(nh) for mi in range(2)]

    def scores(c):
        maxes, steps = [], []
        for hh, mi in streams:
            hd = heads[hh]
            q = q_ref[0, c * t:(c + 1) * t, hd["cols"]]
            keep = (lane < DIFF_HEAD_DIM) if mi == 0 else (lane >= DIFF_HEAD_DIM)
            qm = jnp.where(keep, q, jnp.zeros_like(q))
            maxes.append([])
            steps.append(_score_tiles(hd["k_tile"], jnp.concatenate([qm, hd["q_extra"]], axis=1),
                                      hd["shifts"](c), mask_diag,
                                      s_ref.at[c % slots, 2 * hh + mi], maxes[-1]))
        return _alternate(*steps), maxes

    def finish(c, maxes):
        outs = [[] for _ in streams]
        yield from _alternate(*[
            _softmax_pv(s_ref.at[c % slots, 2 * hh + mi], maxes[si][0], heads[hh]["shifts"](c),
                        heads[hh]["vt_tile"], outs[si])
            for si, (hh, mi) in enumerate(streams)])
        for hh, hd in enumerate(heads):
            o = (outs[2 * hh][0] - lam * outs[2 * hh + 1][0]).T
            o_ref[0, c * t:(c + 1) * t, hd["cols"]] = (
                _rms_normalize(o, gs_ref[...]) * (1.0 - LAM_INIT)).astype(o_ref.dtype)

    _run_pipelined(scores, finish, n_tiles, slots)


MOBA_HEADS_PER_STEP = 2


def _moba_attn_kernel(slopes_ref, q_ref, k_ref, v_ref, *refs, n_blocks, n_sel, n_casts):
    cast_src, o_ref, cast_dst = refs[:n_casts], refs[n_casts], refs[n_casts + 1:2 * n_casts + 1]
    vt_ref, s_ref = refs[2 * n_casts + 1:]
    for src, dst in zip(cast_src, cast_dst):
        dst[...] = src[...].astype(dst.dtype)

    blk, dh, nh, slots = MOBA_BLOCK, MOBA_HEAD_DIM, MOBA_HEADS_PER_STEP, s_ref.shape[0]
    mask_diag = _causal_mask_t(blk)
    blk_id = lax.broadcasted_iota(jnp.int32, (n_blocks, blk), 0)

    def head_setup(hh):
        cols = slice(hh * dh, (hh + 1) * dh)
        slope = slopes_ref[pl.program_id(1) * nh + hh]
        vt_ref[hh] = v_ref[0, :, cols].T
        k_extra, q_extra = _alibi_columns(slope, blk)
        kmean = jnp.mean(k_ref[0, :, cols].astype(jnp.float32).reshape(n_blocks, blk, dh), axis=1)
        kmean = jnp.concatenate([kmean, jnp.zeros((LANES - n_blocks, dh), jnp.float32)], axis=0)
        kmean_hi = kmean.astype(jnp.bfloat16)
        return dict(
            cols=cols, slope=slope, q_extra=q_extra, q_shift=_query_shift(slope, blk),
            kmean_hi=kmean_hi, kmean_lo=(kmean - kmean_hi.astype(jnp.float32)).astype(jnp.bfloat16),
            k_tile=lambda j: jnp.concatenate([k_ref[0, j * blk:(j + 1) * blk, cols], k_extra], axis=1),
            vt_tile=lambda j: vt_ref[hh, :, j * blk:(j + 1) * blk])

    heads = [head_setup(hh) for hh in range(nh)]

    def head_scores(c, hh, hd):
        q = q_ref[0, c * blk:(c + 1) * blk, hd["cols"]]
        g = (lax.dot_general(hd["kmean_hi"], q, _NT_DIMS, preferred_element_type=jnp.float32)
             + lax.dot_general(hd["kmean_lo"], q, _NT_DIMS, preferred_element_type=jnp.float32))[:n_blocks]
        past = blk_id < c
        g = jnp.where(past, g, -jnp.inf)
        rank = jnp.zeros(g.shape, jnp.int32)
        for r in range(1, n_blocks):
            other = pltpu.roll(g, r, 0)
            rank += jnp.where(blk_id >= r, (other >= g).astype(jnp.int32), (other > g).astype(jnp.int32))
        keep = (past & (rank < n_sel)) | (blk_id == c)
        sel_bias = jnp.where(keep, 0.0, MASK_VALUE)
        shifts = [sel_bias[j:j + 1, :] + (hd["q_shift"] - hd["slope"] * float((c - j) * blk))
                  for j in range(c + 1)]
        q_aug = jnp.concatenate([q, hd["q_extra"]], axis=1)
        m = []
        steps = _score_tiles(hd["k_tile"], q_aug, shifts, mask_diag, s_ref.at[c % slots, hh], m)
        return steps, (m, shifts)

    def scores(c):
        per_head = [head_scores(c, hh, hd) for hh, hd in enumerate(heads)]
        return _alternate(*[steps for steps, _ in per_head]), [state for _, state in per_head]

    def finish(c, states):
        outs = [[] for _ in heads]
        yield from _alternate(*[
            _softmax_pv(s_ref.at[c % slots, hh], m[0], shifts, heads[hh]["vt_tile"], outs[hh])
            for hh, (m, shifts) in enumerate(states)])
        for hh, hd in enumerate(heads):
            o_ref[0, c * blk:(c + 1) * blk, hd["cols"]] = outs[hh][0].T.astype(o_ref.dtype)

    _run_pipelined(scores, finish, n_blocks, slots)


MIXERS_SCORE_SLOTS = 2


def _mixers_kernel(slopes_a_ref, slopes_b_ref, lq1_ref, lk1_ref, lq2_ref, lk2_ref, gs_ref,
                   qa_ref, ka_ref, va_ref, qb_ref, kb_ref, vb_ref, *refs, n_blocks, n_sel, n_casts):
    cast_src = refs[:n_casts]
    oa_ref, ob_ref = refs[n_casts], refs[n_casts + 1]
    cast_dst = refs[n_casts + 2:2 * n_casts + 2]
    vta_ref, sa_ref, vtb_ref, sb_ref = refs[2 * n_casts + 2:]
    _diff_attn_kernel(slopes_a_ref, lq1_ref, lk1_ref, lq2_ref, lk2_ref, gs_ref,
                      qa_ref, ka_ref, va_ref, oa_ref, vta_ref, sa_ref)
    _moba_attn_kernel(slopes_b_ref, qb_ref, kb_ref, vb_ref, *cast_src, ob_ref, *cast_dst, vtb_ref, sb_ref,
                      n_blocks=n_blocks, n_sel=n_sel, n_casts=n_casts)


def _mixers(qkv, slopes_a, slopes_b, lam_q1, lam_k1, lam_q2, lam_k2, g_subln, w_in, row_weights):
    b, s, _ = qkv.shape
    hps = ATTN_HEADS_PER_STEP
    assert hps == MOBA_HEADS_PER_STEP and N_HEADS_DIFF == N_HEADS_MOBA
    nh, hw, dh, blk = N_HEADS_DIFF, 2 * DIFF_HEAD_DIM, MOBA_HEAD_DIM, MOBA_BLOCK
    assert hw == LANES and dh == LANES and s % ATTN_TILE == 0 and s % blk == 0 and nh % hps == 0
    n_blocks = s // blk
    assert n_blocks == SUBLANES, "block ranking uses one vreg row per MoBA block"
    n_sel = min(MOBA_TOPK, n_blocks - 1)
    w = hps * LANES
    n_g = nh // hps
    ka0, va0 = DIFF_WIDTH // w, 2 * DIFF_WIDTH // w
    qb0 = 3 * DIFF_WIDTH // w
    kb0, vb0 = qb0 + n_g, qb0 + 2 * n_g
    head_block = lambda first: pl.BlockSpec((1, s, w), lambda bi, g: (bi, 0, first + g))
    lam_spec = pl.BlockSpec((1, DIFF_HEAD_DIM), lambda bi, g: (0, 0))

    n_steps = b * n_g
    step = lambda bi, g: bi * n_g + g
    d, gate_cols = w_in.shape[0], w_in.shape[1] - QKV_COLS
    gc = gate_cols // n_steps
    assert gate_cols % n_steps == 0 and gc % LANES == 0 and QKV_COLS % gc == 0
    cast_in = [pl.BlockSpec((d, gc), lambda bi, g: (0, QKV_COLS // gc + step(bi, g)))]
    cast_out = [pl.BlockSpec((d, gc), lambda bi, g: (0, step(bi, g)))]
    cast_shapes = [jax.ShapeDtypeStruct((d, gate_cols), jnp.bfloat16)]
    for m in row_weights:
        rows = m.shape[0] // n_steps
        assert m.shape[0] % n_steps == 0 and rows % (2 * SUBLANES) == 0
        spec = pl.BlockSpec((rows, m.shape[1]), lambda bi, g: (step(bi, g), 0))
        cast_in.append(spec)
        cast_out.append(spec)
        cast_shapes.append(jax.ShapeDtypeStruct(m.shape, jnp.bfloat16))

    kern = functools.partial(_mixers_kernel, n_blocks=n_blocks, n_sel=n_sel, n_casts=len(cast_in))
    o_a, o_b, w_gates, *row_weights_bf = pl.pallas_call(
        kern,
        grid=(b, n_g),
        in_specs=[
            pl.BlockSpec(memory_space=pltpu.SMEM), pl.BlockSpec(memory_space=pltpu.SMEM),
            lam_spec, lam_spec, lam_spec, lam_spec,
            pl.BlockSpec((1, hw), lambda bi, g: (0, 0)),
            head_block(0), head_block(ka0), head_block(va0),
            head_block(qb0), head_block(kb0), head_block(vb0),
        ] + cast_in,
        out_specs=[head_block(0), head_block(0)] + cast_out,
        out_shape=[jax.ShapeDtypeStruct((b, s, DIFF_WIDTH), jnp.bfloat16),
                   jax.ShapeDtypeStruct((b, s, MOBA_WIDTH), jnp.bfloat16)] + cast_shapes,
        scratch_shapes=[pltpu.VMEM((hps, hw, s), jnp.bfloat16),
                        pltpu.VMEM((MIXERS_SCORE_SLOTS, 2 * hps, s, ATTN_TILE), jnp.float32),
                        pltpu.VMEM((hps, dh, s), jnp.bfloat16),
                        pltpu.VMEM((MIXERS_SCORE_SLOTS, hps, s, blk), jnp.float32)],
        compiler_params=_compiler_params(("parallel", "parallel")),
        name="mixers",
    )(slopes_a, slopes_b, lam_q1, lam_k1, lam_q2, lam_k2, g_subln, qkv, qkv, qkv, qkv, qkv, qkv,
      w_in, *row_weights)
    return o_a, o_b, w_gates, row_weights_bf


def _mix_out_kernel(x_hbm, g_ref, oa_ref, ob_ref, wga_ref, wgb_ref, pa_ref, pb_ref, wo_ref, o_ref,
                    h_ref, x_buf, x_sem):
    def start_tile():
        x = x_buf[...]
        h_ref[...] = _rms_normalize(x, g_ref[...]).astype(jnp.bfloat16)
        o_ref[...] = x

    _on_token_tile(x_hbm, x_buf, x_sem, start_tile)

    h = h_ref[...]
    gate_a = jnp.dot(h, wga_ref[...], preferred_element_type=jnp.float32)
    gate_b = jnp.dot(h, wgb_ref[...], preferred_element_type=jnp.float32)
    proj_a = jnp.dot(oa_ref[...], pa_ref[...], preferred_element_type=jnp.float32)
    proj_b = jnp.dot(ob_ref[...], pb_ref[...], preferred_element_type=jnp.float32)
    merged = jax.nn.sigmoid(gate_a) * proj_a + jax.nn.sigmoid(gate_b) * proj_b
    o_ref[...] += jnp.dot(merged.astype(jnp.bfloat16), wo_ref[...], preferred_element_type=jnp.float32)


def _mix_out(x, gain, o_a, o_b, w_gates, p_a, p_b, w_o):
    t, d = x.shape
    tm, tc = MIX_TOKEN_TILE, MIX_COL_TILE
    assert t % tm == 0 and d % tc == 0 and d // tc >= 2
    ga_blk0, gb_blk0 = 0, d // tc
    return pl.pallas_call(
        _mix_out_kernel,
        grid=(t // tm, d // tc),
        in_specs=[
            _X_IN_HBM,
            pl.BlockSpec((1, d), lambda i, c: (0, 0)),
            pl.BlockSpec((tm, DIFF_WIDTH), lambda i, c: (i, 0)),
            pl.BlockSpec((tm, MOBA_WIDTH), lambda i, c: (i, 0)),
            pl.BlockSpec((d, tc), lambda i, c: (0, ga_blk0 + c)),
            pl.BlockSpec((d, tc), lambda i, c: (0, gb_blk0 + c)),
            pl.BlockSpec((DIFF_WIDTH, tc), lambda i, c: (0, c)),
            pl.BlockSpec((MOBA_WIDTH, tc), lambda i, c: (0, c)),
            pl.BlockSpec((tc, d), lambda i, c: (c, 0)),
        ],
        out_specs=pl.BlockSpec((tm, d), lambda i, c: (i, 0)),
        out_shape=jax.ShapeDtypeStruct((t, d), jnp.float32),
        scratch_shapes=[pltpu.VMEM((tm, d), jnp.bfloat16)] + _token_tile_scratch(tm, d),
        compiler_params=_compiler_params(_SEQUENTIAL_GRID),
        name="mix_out",
    )(x, gain, o_a, o_b, w_gates, w_gates, p_a, p_b, w_o)


LOG2_E = math.log2(math.e)


def _alibi_slopes(n):
    return jnp.asarray(LOG2_E * 2.0 ** (-8.0 * np.arange(1, n + 1) / n), dtype=jnp.float32)


def _qkv_col_scale():
    scale = np.ones((1, QKV_COLS), np.float32)
    scale[:, :DIFF_WIDTH] = LOG2_E * DIFF_HEAD_DIM ** -0.5
    scale[:, 3 * DIFF_WIDTH:3 * DIFF_WIDTH + MOBA_WIDTH] = LOG2_E * MOBA_HEAD_DIM ** -0.5
    return jnp.asarray(scale)


def kernel(x, g_ffn1, w_ffn1_gu, w_ffn1_down, g_mix, w_in, lam_q1, lam_k1, lam_q2, lam_k2, g_subln, p_a, p_b, w_o, g_ffn2, w_ffn2_gu, w_ffn2_down, g_final):
    b, s, d = x.shape
    assert g_ffn1.shape[0] == 1, "single-layer stack"
    xt = x.reshape(b * s, d)
    g_final_row = g_final.reshape(1, d)

    x1 = _ffn_two_phase(xt, g_ffn1, w_ffn1_gu[0], w_ffn1_down[0])

    qkv = _qkv_proj(x1, g_mix, w_in[0], _qkv_col_scale()).reshape(b, s, QKV_COLS)
    o_a, o_b, w_gates, (p_a_bf, p_b_bf, w_o_bf) = _mixers(
        qkv, _alibi_slopes(N_HEADS_DIFF), _alibi_slopes(N_HEADS_MOBA), lam_q1, lam_k1, lam_q2, lam_k2,
        g_subln, w_in[0], (p_a[0], p_b[0], w_o[0]))
    x2 = _mix_out(x1, g_mix, o_a.reshape(b * s, DIFF_WIDTH), o_b.reshape(b * s, MOBA_WIDTH),
                  w_gates, p_a_bf, p_b_bf, w_o_bf)

    out = _ffn(x2, g_ffn2, w_ffn2_gu[0], w_ffn2_down[0], g_final_row, final_norm=True)
    return out.reshape(b, s, d)
```

```python
import functools
import itertools
import math

import jax
import jax.numpy as jnp
import numpy as np
from jax import lax
from jax.experimental import pallas as pl
from jax.experimental.pallas import tpu as pltpu

D_MODEL = 2048
N_HEADS_DIFF = 8
DIFF_HEAD_DIM = 64
DIFF_WIDTH = N_HEADS_DIFF * 2 * DIFF_HEAD_DIM
N_HEADS_MOBA = 8
MOBA_HEAD_DIM = 128
MOBA_WIDTH = N_HEADS_MOBA * MOBA_HEAD_DIM
MOBA_BLOCK = 256
MOBA_TOPK = 3
D_FF = 5632
RMS_EPS = 1e-6
QKV_COLS = 3 * DIFF_WIDTH + 3 * MOBA_WIDTH
LAM_INIT = 0.8 - 0.6 * math.exp(-0.3 * 0)

LANES = 128
SUBLANES = 8
VMEM_LIMIT_BYTES = 63 * 1024 * 1024
MASK_VALUE = -1e30

FFN_TOKEN_TILE = 1024
FFN_FF_TILE = 512
PROJ_TOKEN_TILE = 2048
PROJ_COL_TILE = 1024
MIX_TOKEN_TILE = 1024
MIX_COL_TILE = 512
ATTN_TILE = 256

_NT_DIMS = (((1,), (1,)), ((), ()))


def _rms_normalize(x, gain):
    ms = jnp.mean(x * x, axis=-1, keepdims=True)
    return x * lax.rsqrt(ms + RMS_EPS) * gain


def _compiler_params(semantics):
    return pltpu.CompilerParams(dimension_semantics=semantics, vmem_limit_bytes=VMEM_LIMIT_BYTES)


def _on_token_tile(x_hbm, x_buf, x_sem, consume):
    i = pl.program_id(0)
    j = pl.program_id(1)
    tm = x_buf.shape[0]

    def copy(tile):
        return pltpu.make_async_copy(x_hbm.at[pl.ds(tile * tm, tm), :], x_buf, x_sem)

    @pl.when((i == 0) & (j == 0))
    def _():
        copy(0).start()

    @pl.when(j == 0)
    def _():
        copy(i).wait()
        consume()

    @pl.when((j == 1) & (i + 1 < pl.num_programs(0)))
    def _():
        copy(i + 1).start()


_X_IN_HBM = pl.BlockSpec(memory_space=pl.ANY)
_SEQUENTIAL_GRID = ("arbitrary", "arbitrary")


def _token_tile_scratch(tm, d):
    return [pltpu.VMEM((tm, d), jnp.float32), pltpu.SemaphoreType.DMA(())]


def _ffn_kernel(x_hbm, g_ref, wg_ref, wu_ref, wd_ref, gf_ref, o_ref, h_ref, x_buf, x_sem,
                *, n_ff_tiles, final_norm):
    f = pl.program_id(1)
    bf16 = jnp.bfloat16
    last = n_ff_tiles - 1

    def half_swiglu():
        h = h_ref[...]
        gate = jnp.dot(h, wg_ref[...].astype(bf16), preferred_element_type=jnp.float32)
        up = jnp.dot(h, wu_ref[...].astype(bf16), preferred_element_type=jnp.float32)
        act = (gate * jax.nn.sigmoid(gate) * up * 0.5).astype(bf16)
        return jnp.dot(act, wd_ref[...].astype(bf16), preferred_element_type=jnp.float32)

    def first_step():
        x = x_buf[...]
        h_ref[...] = _rms_normalize(x, g_ref[...]).astype(bf16)
        o_ref[...] = x + half_swiglu()

    _on_token_tile(x_hbm, x_buf, x_sem, first_step)

    @pl.when((f > 0) & (f < last) if final_norm else (f > 0))
    def _():
        o_ref[...] += half_swiglu()

    if final_norm:
        @pl.when(f == last)
        def _():
            o_ref[...] = _rms_normalize(o_ref[...] + half_swiglu(), gf_ref[...])


def _ffn(x, gain, w_gu, w_down, final_gain, *, final_norm):
    t, d = x.shape
    d_ff = w_down.shape[0]
    tm, tf = FFN_TOKEN_TILE, FFN_FF_TILE
    n_ff_tiles = d_ff // tf
    assert t % tm == 0 and d_ff % tf == 0 and n_ff_tiles >= 2
    kern = functools.partial(_ffn_kernel, n_ff_tiles=n_ff_tiles, final_norm=final_norm)
    return pl.pallas_call(
        kern,
        grid=(t // tm, n_ff_tiles),
        in_specs=[
            _X_IN_HBM,
            pl.BlockSpec((1, d), lambda i, f: (0, 0)),
            pl.BlockSpec((d, tf), lambda i, f: (0, f)),
            pl.BlockSpec((d, tf), lambda i, f: (0, f + n_ff_tiles)),
            pl.BlockSpec((tf, d), lambda i, f: (f, 0)),
            pl.BlockSpec((1, d), lambda i, f: (0, 0)),
        ],
        out_specs=pl.BlockSpec((tm, d), lambda i, f: (i, 0)),
        out_shape=jax.ShapeDtypeStruct((t, d), jnp.float32),
        scratch_shapes=[pltpu.VMEM((tm, d), jnp.bfloat16)] + _token_tile_scratch(tm, d),
        compiler_params=_compiler_params(_SEQUENTIAL_GRID),
        name="ffn_final" if final_norm else "ffn",
    )(x, gain, w_gu, w_gu, w_down, final_gain)


def _qkv_proj_kernel(x_hbm, g_ref, w_ref, s_ref, o_ref, h_ref, x_buf, x_sem):
    def start_tile():
        h_ref[...] = _rms_normalize(x_buf[...], g_ref[...]).astype(jnp.bfloat16)

    _on_token_tile(x_hbm, x_buf, x_sem, start_tile)

    acc = jnp.dot(h_ref[...], w_ref[...].astype(jnp.bfloat16), preferred_element_type=jnp.float32)
    o_ref[...] = (acc * s_ref[...]).astype(o_ref.dtype)


def _qkv_proj(x, gain, w_in, col_scale):
    t, d = x.shape
    tm, tn = PROJ_TOKEN_TILE, PROJ_COL_TILE
    assert t % tm == 0 and QKV_COLS % tn == 0 and QKV_COLS // tn >= 2
    return pl.pallas_call(
        _qkv_proj_kernel,
        grid=(t // tm, QKV_COLS // tn),
        in_specs=[
            _X_IN_HBM,
            pl.BlockSpec((1, d), lambda i, j: (0, 0)),
            pl.BlockSpec((d, tn), lambda i, j: (0, j)),
            pl.BlockSpec((1, tn), lambda i, j: (0, j)),
        ],
        out_specs=pl.BlockSpec((tm, tn), lambda i, j: (i, j)),
        out_shape=jax.ShapeDtypeStruct((t, QKV_COLS), jnp.bfloat16),
        scratch_shapes=[pltpu.VMEM((tm, d), jnp.bfloat16)] + _token_tile_scratch(tm, d),
        compiler_params=_compiler_params(_SEQUENTIAL_GRID),
        name="qkv_proj",
    )(x, gain, w_in, col_scale)


ALIBI_TERMS = 3


def _alibi_columns(slope, t):
    lane = lax.broadcasted_iota(jnp.int32, (t, LANES), 1)
    rest = lax.broadcasted_iota(jnp.int32, (t, LANES), 0).astype(jnp.float32) * slope
    k_extra = jnp.zeros((t, LANES), jnp.float32)
    for i in range(ALIBI_TERMS):
        term = rest.astype(jnp.bfloat16).astype(jnp.float32)
        k_extra = jnp.where(lane == i, term, k_extra)
        rest = rest - term
    q_extra = jnp.where(lane < ALIBI_TERMS, 1.0, 0.0)
    return k_extra.astype(jnp.bfloat16), q_extra.astype(jnp.bfloat16)


def _query_shift(slope, t):
    return lax.broadcasted_iota(jnp.int32, (1, t), 1).astype(jnp.float32) * (-slope)


def _causal_mask_t(t):
    kk = lax.broadcasted_iota(jnp.int32, (t, t), 0)
    qq = lax.broadcasted_iota(jnp.int32, (t, t), 1)
    return jnp.where(qq >= kk, 0.0, MASK_VALUE)


def _fold_rows(x, op):
    rows, cols = x.shape
    return op(x.reshape(rows // SUBLANES, SUBLANES, cols), axis=0)


def _score_tiles(k_tile, q, shifts, mask_diag, s_ref, result):
    n, t = len(shifts), q.shape[0]
    cand = None
    for j in range(n):
        sj = lax.dot_general(k_tile(j), q, _NT_DIMS, preferred_element_type=jnp.float32)
        if j == n - 1:
            sj = sj + mask_diag
        s_ref[j * t:(j + 1) * t, :] = sj
        cj = _fold_rows(sj, jnp.max) + shifts[j]
        cand = cj if cand is None else jnp.maximum(cand, cj)
        yield
    result.append(jnp.max(cand, axis=0, keepdims=True))


ONES_ROWS = 2 * SUBLANES


def _softmax_pv(s_ref, m, shifts, vt_tile, result, dv=None):
    t = m.shape[1]
    lpart, acc = None, None
    for j in range(len(shifts)):
        sj = s_ref[j * t:(j + 1) * t, :]
        p = jnp.exp2(sj - (m - shifts[j]))
        if dv is None:
            pj = _fold_rows(p, jnp.sum)
            lpart = pj if lpart is None else lpart + pj
        vt = vt_tile(j)
        pv = jnp.dot(vt, p.astype(vt.dtype), preferred_element_type=jnp.float32)
        acc = pv if acc is None else acc + pv
        yield
    if dv is None:
        result.append(acc / jnp.sum(lpart, axis=0, keepdims=True))
    else:
        result.append(acc[:dv] / acc[dv:dv + 1])


def _alternate(*steps):
    for _ in itertools.zip_longest(*steps):
        yield


def _run_pipelined(stage_a, stage_b, n, slots):
    ahead = slots - 1
    states = {}

    def run_a(u):
        if u < n:
            steps, states[u] = stage_a(u)
            for _ in steps:
                pass

    for u in range(ahead):
        run_a(u)
    for u in range(n):
        run_a(u + ahead)
        for _ in stage_b(u, states.pop(u)):
            pass


ATTN_HEADS_PER_STEP = 2


def _diff_attn_kernel(slopes_ref, lq1_ref, lk1_ref, lq2_ref, lk2_ref, gs_ref,
                      q_ref, k_ref, v_ref, o_ref, vt_ref, s_ref):
    t, hw, nh, slots = ATTN_TILE, 2 * DIFF_HEAD_DIM, ATTN_HEADS_PER_STEP, s_ref.shape[0]
    n_tiles = q_ref.shape[1] // t
    mask_diag = _causal_mask_t(t)
    lam = (jnp.exp(jnp.sum(lq1_ref[...] * lk1_ref[...], axis=-1, keepdims=True))
           - jnp.exp(jnp.sum(lq2_ref[...] * lk2_ref[...], axis=-1, keepdims=True))
           + LAM_INIT)
    lane = lax.broadcasted_iota(jnp.int32, (t, LANES), 1)

    def head_setup(hh):
        cols = slice(hh * hw, (hh + 1) * hw)
        slope = slopes_ref[pl.program_id(1) * nh + hh]
        vt_ref[hh, :hw] = v_ref[0, :, cols].T
        vt_ref[hh, hw:] = jnp.ones((ONES_ROWS, vt_ref.shape[2]), vt_ref.dtype)
        k_extra, q_extra = _alibi_columns(slope, t)
        q_shift = _query_shift(slope, t)
        return dict(
            cols=cols, q_extra=q_extra,
            shifts=lambda c: [q_shift - slope * float((c - j) * t) for j in range(c + 1)],
            k_tile=lambda j: jnp.concatenate([k_ref[0, j * t:(j + 1) * t, cols], k_extra], axis=1),
            vt_tile=lambda j: vt_ref[hh, :, j * t:(j + 1) * t])

    heads = [head_setup(hh) for hh in range(nh)]
    streams = [(hh, mi) for hh in range(nh) for mi in range(2)]

    def scores(c):
        maxes, steps = [], []
        for hh, mi in streams:
            hd = heads[hh]
            q = q_ref[0, c * t:(c + 1) * t, hd["cols"]]
            keep = (lane < DIFF_HEAD_DIM) if mi == 0 else (lane >= DIFF_HEAD_DIM)
            qm = jnp.where(keep, q, jnp.zeros_like(q))
            maxes.append([])
            steps.append(_score_tiles(hd["k_tile"], jnp.concatenate([qm, hd["q_extra"]], axis=1),
                                      hd["shifts"](c), mask_diag,
                                      s_ref.at[c % slots, 2 * hh + mi], maxes[-1]))
        return _alternate(*steps), maxes

    def finish(c, maxes):
        outs = [[] for _ in streams]
        yield from _alternate(*[
            _softmax_pv(s_ref.at[c % slots, 2 * hh + mi], maxes[si][0], heads[hh]["shifts"](c),
                        heads[hh]["vt_tile"], outs[si], dv=hw)
            for si, (hh, mi) in enumerate(streams)])
        for hh, hd in enumerate(heads):
            o = (outs[2 * hh][0] - lam * outs[2 * hh + 1][0]).T
            o_ref[0, c * t:(c + 1) * t, hd["cols"]] = (
                _rms_normalize(o, gs_ref[...]) * (1.0 - LAM_INIT)).astype(o_ref.dtype)

    _run_pipelined(scores, finish, n_tiles, slots)


MOBA_HEADS_PER_STEP = 2


def _moba_attn_kernel(slopes_ref, q_ref, k_ref, v_ref, *refs, n_blocks, n_sel, n_casts):
    cast_src, o_ref, cast_dst = refs[:n_casts], refs[n_casts], refs[n_casts + 1:2 * n_casts + 1]
    vt_ref, s_ref = refs[2 * n_casts + 1:]
    for src, dst in zip(cast_src, cast_dst):
        dst[...] = src[...].astype(dst.dtype)

    blk, dh, nh, slots = MOBA_BLOCK, MOBA_HEAD_DIM, MOBA_HEADS_PER_STEP, s_ref.shape[0]
    mask_diag = _causal_mask_t(blk)
    blk_id = lax.broadcasted_iota(jnp.int32, (n_blocks, blk), 0)

    def head_setup(hh):
        cols = slice(hh * dh, (hh + 1) * dh)
        slope = slopes_ref[pl.program_id(1) * nh + hh]
        vt_ref[hh] = v_ref[0, :, cols].T
        k_extra, q_extra = _alibi_columns(slope, blk)
        kmean = jnp.mean(k_ref[0, :, cols].astype(jnp.float32).reshape(n_blocks, blk, dh), axis=1)
        kmean = jnp.concatenate([kmean, jnp.zeros((LANES - n_blocks, dh), jnp.float32)], axis=0)
        kmean_hi = kmean.astype(jnp.bfloat16)
        return dict(
            cols=cols, slope=slope, q_extra=q_extra, q_shift=_query_shift(slope, blk),
            kmean_hi=kmean_hi, kmean_lo=(kmean - kmean_hi.astype(jnp.float32)).astype(jnp.bfloat16),
            k_tile=lambda j: jnp.concatenate([k_ref[0, j * blk:(j + 1) * blk, cols], k_extra], axis=1),
            vt_tile=lambda j: vt_ref[hh, :, j * blk:(j + 1) * blk])

    heads = [head_setup(hh) for hh in range(nh)]

    def head_scores(c, hh, hd):
        q = q_ref[0, c * blk:(c + 1) * blk, hd["cols"]]
        g = (lax.dot_general(hd["kmean_hi"], q, _NT_DIMS, preferred_element_type=jnp.float32)
             + lax.dot_general(hd["kmean_lo"], q, _NT_DIMS, preferred_element_type=jnp.float32))[:n_blocks]
        past = blk_id < c
        g = jnp.where(past, g, -jnp.inf)
        rank = jnp.zeros(g.shape, jnp.int32)
        for r in range(1, n_blocks):
            other = pltpu.roll(g, r, 0)
            rank += jnp.where(blk_id >= r, (other >= g).astype(jnp.int32), (other > g).astype(jnp.int32))
        keep = (past & (rank < n_sel)) | (blk_id == c)
        sel_bias = jnp.where(keep, 0.0, MASK_VALUE)
        shifts = [sel_bias[j:j + 1, :] + (hd["q_shift"] - hd["slope"] * float((c - j) * blk))
                  for j in range(c + 1)]
        q_aug = jnp.concatenate([q, hd["q_extra"]], axis=1)
        m = []
        steps = _score_tiles(hd["k_tile"], q_aug, shifts, mask_diag, s_ref.at[c % slots, hh], m)
        return steps, (m, shifts)

    def scores(c):
        per_head = [head_scores(c, hh, hd) for hh, hd in enumerate(heads)]
        return _alternate(*[steps for steps, _ in per_head]), [state for _, state in per_head]

    def finish(c, states):
        outs = [[] for _ in heads]
        yield from _alternate(*[
            _softmax_pv(s_ref.at[c % slots, hh], m[0], shifts, heads[hh]["vt_tile"], outs[hh])
            for hh, (m, shifts) in enumerate(states)])
        for hh, hd in enumerate(heads):
            o_ref[0, c * blk:(c + 1) * blk, hd["cols"]] = outs[hh][0].T.astype(o_ref.dtype)

    _run_pipelined(scores, finish, n_blocks, slots)


MIXERS_SCORE_SLOTS = 2


def _mixers_kernel(slopes_a_ref, slopes_b_ref, lq1_ref, lk1_ref, lq2_ref, lk2_ref, gs_ref,
                   qa_ref, ka_ref, va_ref, qb_ref, kb_ref, vb_ref, *refs, n_blocks, n_sel, n_casts):
    cast_src = refs[:n_casts]
    oa_ref, ob_ref = refs[n_casts], refs[n_casts + 1]
    cast_dst = refs[n_casts + 2:2 * n_casts + 2]
    vta_ref, sa_ref, vtb_ref, sb_ref = refs[2 * n_casts + 2:]
    _diff_attn_kernel(slopes_a_ref, lq1_ref, lk1_ref, lq2_ref, lk2_ref, gs_ref,
                      qa_ref, ka_ref, va_ref, oa_ref, vta_ref, sa_ref)
    _moba_attn_kernel(slopes_b_ref, qb_ref, kb_ref, vb_ref, *cast_src, ob_ref, *cast_dst, vtb_ref, sb_ref,
                      n_blocks=n_blocks, n_sel=n_sel, n_casts=n_casts)


def _mixers(qkv, slopes_a, slopes_b, lam_q1, lam_k1, lam_q2, lam_k2, g_subln, w_in, row_weights):
    b, s, _ = qkv.shape
    hps = ATTN_HEADS_PER_STEP
    assert hps == MOBA_HEADS_PER_STEP and N_HEADS_DIFF == N_HEADS_MOBA
    nh, hw, dh, blk = N_HEADS_DIFF, 2 * DIFF_HEAD_DIM, MOBA_HEAD_DIM, MOBA_BLOCK
    assert hw == LANES and dh == LANES and s % ATTN_TILE == 0 and s % blk == 0 and nh % hps == 0
    n_blocks = s // blk
    assert n_blocks == SUBLANES, "block ranking uses one vreg row per MoBA block"
    n_sel = min(MOBA_TOPK, n_blocks - 1)
    w = hps * LANES
    n_g = nh // hps
    ka0, va0 = DIFF_WIDTH // w, 2 * DIFF_WIDTH // w
    qb0 = 3 * DIFF_WIDTH // w
    kb0, vb0 = qb0 + n_g, qb0 + 2 * n_g
    head_block = lambda first: pl.BlockSpec((1, s, w), lambda bi, g: (bi, 0, first + g))
    lam_spec = pl.BlockSpec((1, DIFF_HEAD_DIM), lambda bi, g: (0, 0))

    n_steps = b * n_g
    step = lambda bi, g: bi * n_g + g
    d, gate_cols = w_in.shape[0], w_in.shape[1] - QKV_COLS
    gc = gate_cols // n_steps
    assert gate_cols % n_steps == 0 and gc % LANES == 0 and QKV_COLS % gc == 0
    cast_in = [pl.BlockSpec((d, gc), lambda bi, g: (0, QKV_COLS // gc + step(bi, g)))]
    cast_out = [pl.BlockSpec((d, gc), lambda bi, g: (0, step(bi, g)))]
    cast_shapes = [jax.ShapeDtypeStruct((d, gate_cols), jnp.bfloat16)]
    for m in row_weights:
        rows = m.shape[0] // n_steps
        assert m.shape[0] % n_steps == 0 and rows % (2 * SUBLANES) == 0
        spec = pl.BlockSpec((rows, m.shape[1]), lambda bi, g: (step(bi, g), 0))
        cast_in.append(spec)
        cast_out.append(spec)
        cast_shapes.append(jax.ShapeDtypeStruct(m.shape, jnp.bfloat16))

    kern = functools.partial(_mixers_kernel, n_blocks=n_blocks, n_sel=n_sel, n_casts=len(cast_in))
    o_a, o_b, w_gates, *row_weights_bf = pl.pallas_call(
        kern,
        grid=(b, n_g),
        in_specs=[
            pl.BlockSpec(memory_space=pltpu.SMEM), pl.BlockSpec(memory_space=pltpu.SMEM),
            lam_spec, lam_spec, lam_spec, lam_spec,
            pl.BlockSpec((1, hw), lambda bi, g: (0, 0)),
            head_block(0), head_block(ka0), head_block(va0),
            head_block(qb0), head_block(kb0), head_block(vb0),
        ] + cast_in,
        out_specs=[head_block(0), head_block(0)] + cast_out,
        out_shape=[jax.ShapeDtypeStruct((b, s, DIFF_WIDTH), jnp.bfloat16),
                   jax.ShapeDtypeStruct((b, s, MOBA_WIDTH), jnp.bfloat16)] + cast_shapes,
        scratch_shapes=[pltpu.VMEM((hps, hw + ONES_ROWS, s), jnp.bfloat16),
                        pltpu.VMEM((MIXERS_SCORE_SLOTS, 2 * hps, s, ATTN_TILE), jnp.float32),
                        pltpu.VMEM((hps, dh, s), jnp.bfloat16),
                        pltpu.VMEM((MIXERS_SCORE_SLOTS, hps, s, blk), jnp.float32)],
        compiler_params=_compiler_params(("parallel", "parallel")),
        name="mixers",
    )(slopes_a, slopes_b, lam_q1, lam_k1, lam_q2, lam_k2, g_subln, qkv, qkv, qkv, qkv, qkv, qkv,
      w_in, *row_weights)
    return o_a, o_b, w_gates, row_weights_bf


def _mix_out_kernel(x_hbm, g_ref, oa_ref, ob_ref, wga_ref, wgb_ref, pa_ref, pb_ref, wo_ref, o_ref,
                    h_ref, x_buf, x_sem):
    def start_tile():
        x = x_buf[...]
        h_ref[...] = _rms_normalize(x, g_ref[...]).astype(jnp.bfloat16)
        o_ref[...] = x

    _on_token_tile(x_hbm, x_buf, x_sem, start_tile)

    h = h_ref[...]
    gate_a = jnp.dot(h, wga_ref[...], preferred_element_type=jnp.float32)
    gate_b = jnp.dot(h, wgb_ref[...], preferred_element_type=jnp.float32)
    proj_a = jnp.dot(oa_ref[...], pa_ref[...], preferred_element_type=jnp.float32)
    proj_b = jnp.dot(ob_ref[...], pb_ref[...], preferred_element_type=jnp.float32)
    merged = jax.nn.sigmoid(gate_a) * proj_a + jax.nn.sigmoid(gate_b) * proj_b
    o_ref[...] += jnp.dot(merged.astype(jnp.bfloat16), wo_ref[...], preferred_element_type=jnp.float32)


def _mix_out(x, gain, o_a, o_b, w_gates, p_a, p_b, w_o):
    t, d = x.shape
    tm, tc = MIX_TOKEN_TILE, MIX_COL_TILE
    assert t % tm == 0 and d % tc == 0 and d // tc >= 2
    ga_blk0, gb_blk0 = 0, d // tc
    return pl.pallas_call(
        _mix_out_kernel,
        grid=(t // tm, d // tc),
        in_specs=[
            _X_IN_HBM,
            pl.BlockSpec((1, d), lambda i, c: (0, 0)),
            pl.BlockSpec((tm, DIFF_WIDTH), lambda i, c: (i, 0)),
            pl.BlockSpec((tm, MOBA_WIDTH), lambda i, c: (i, 0)),
            pl.BlockSpec((d, tc), lambda i, c: (0, ga_blk0 + c)),
            pl.BlockSpec((d, tc), lambda i, c: (0, gb_blk0 + c)),
            pl.BlockSpec((DIFF_WIDTH, tc), lambda i, c: (0, c)),
            pl.BlockSpec((MOBA_WIDTH, tc), lambda i, c: (0, c)),
            pl.BlockSpec((tc, d), lambda i, c: (c, 0)),
        ],
        out_specs=pl.BlockSpec((tm, d), lambda i, c: (i, 0)),
        out_shape=jax.ShapeDtypeStruct((t, d), jnp.float32),
        scratch_shapes=[pltpu.VMEM((tm, d), jnp.bfloat16)] + _token_tile_scratch(tm, d),
        compiler_params=_compiler_params(_SEQUENTIAL_GRID),
        name="mix_out",
    )(x, gain, o_a, o_b, w_gates, w_gates, p_a, p_b, w_o)


LOG2_E = math.log2(math.e)


def _alibi_slopes(n):
    return jnp.asarray(LOG2_E * 2.0 ** (-8.0 * np.arange(1, n + 1) / n), dtype=jnp.float32)


def _qkv_col_scale():
    scale = np.ones((1, QKV_COLS), np.float32)
    scale[:, :DIFF_WIDTH] = LOG2_E * DIFF_HEAD_DIM ** -0.5
    scale[:, 3 * DIFF_WIDTH:3 * DIFF_WIDTH + MOBA_WIDTH] = LOG2_E * MOBA_HEAD_DIM ** -0.5
    return jnp.asarray(scale)


def kernel(x, g_ffn1, w_ffn1_gu, w_ffn1_down, g_mix, w_in, lam_q1, lam_k1, lam_q2, lam_k2, g_subln, p_a, p_b, w_o, g_ffn2, w_ffn2_gu, w_ffn2_down, g_final):
    b, s, d = x.shape
    assert g_ffn1.shape[0] == 1, "single-layer stack"
    xt = x.reshape(b * s, d)
    g_final_row = g_final.reshape(1, d)

    x1 = _ffn(xt, g_ffn1, w_ffn1_gu[0], w_ffn1_down[0], g_final_row, final_norm=False)

    qkv = _qkv_proj(x1, g_mix, w_in[0], _qkv_col_scale()).reshape(b, s, QKV_COLS)
    o_a, o_b, w_gates, (p_a_bf, p_b_bf, w_o_bf) = _mixers(
        qkv, _alibi_slopes(N_HEADS_DIFF), _alibi_slopes(N_HEADS_MOBA), lam_q1, lam_k1, lam_q2, lam_k2,
        g_subln, w_in[0], (p_a[0], p_b[0], w_o[0]))
    x2 = _mix_out(x1, g_mix, o_a.reshape(b * s, DIFF_WIDTH), o_b.reshape(b * s, MOBA_WIDTH),
                  w_gates, p_a_bf, p_b_bf, w_o_bf)

    out = _ffn(x2, g_ffn2, w_ffn2_gu[0], w_ffn2_down[0], g_final_row, final_norm=True)
    return out.reshape(b, s, d)
```

```python
import functools
import itertools
import math

import jax
import jax.numpy as jnp
import numpy as np
from jax import lax
from jax.experimental import pallas as pl
from jax.experimental.pallas import tpu as pltpu

D_MODEL = 2048
N_HEADS_DIFF = 8
DIFF_HEAD_DIM = 64
DIFF_WIDTH = N_HEADS_DIFF * 2 * DIFF_HEAD_DIM
N_HEADS_MOBA = 8
MOBA_HEAD_DIM = 128
MOBA_WIDTH = N_HEADS_MOBA * MOBA_HEAD_DIM
MOBA_BLOCK = 256
MOBA_TOPK = 3
D_FF = 5632
RMS_EPS = 1e-6
QKV_COLS = 3 * DIFF_WIDTH + 3 * MOBA_WIDTH
LAM_INIT = 0.8 - 0.6 * math.exp(-0.3 * 0)

LANES = 128
SUBLANES = 8
VMEM_LIMIT_BYTES = 63 * 1024 * 1024
MASK_VALUE = -1e30

FFN_TOKEN_TILE = 1024
FFN_FF_TILE = 512
PROJ_TOKEN_TILE = 2048
PROJ_COL_TILE = 1024
MIX_TOKEN_TILE = 1024
MIX_COL_TILE = 512
ATTN_TILE = 256

_NT_DIMS = (((1,), (1,)), ((), ()))


def _rms_normalize(x, gain):
    ms = jnp.mean(x * x, axis=-1, keepdims=True)
    return x * lax.rsqrt(ms + RMS_EPS) * gain


def _compiler_params(semantics):
    return pltpu.CompilerParams(dimension_semantics=semantics, vmem_limit_bytes=VMEM_LIMIT_BYTES)


def _on_token_tile(x_hbm, x_buf, x_sem, consume):
    i = pl.program_id(0)
    j = pl.program_id(1)
    tm = x_buf.shape[0]

    def copy(tile):
        return pltpu.make_async_copy(x_hbm.at[pl.ds(tile * tm, tm), :], x_buf, x_sem)

    @pl.when((i == 0) & (j == 0))
    def _():
        copy(0).start()

    @pl.when(j == 0)
    def _():
        copy(i).wait()
        consume()

    @pl.when((j == 1) & (i + 1 < pl.num_programs(0)))
    def _():
        copy(i + 1).start()


_X_IN_HBM = pl.BlockSpec(memory_space=pl.ANY)
_SEQUENTIAL_GRID = ("arbitrary", "arbitrary")


def _token_tile_scratch(tm, d):
    return [pltpu.VMEM((tm, d), jnp.float32), pltpu.SemaphoreType.DMA(())]


def _ffn_kernel(x_hbm, g_ref, wg_ref, wu_ref, wd_ref, gf_ref, o_ref, h_ref, x_buf, x_sem,
                *, n_ff_tiles, final_norm):
    f = pl.program_id(1)
    bf16 = jnp.bfloat16
    last = n_ff_tiles - 1

    def half_swiglu():
        h = h_ref[...]
        gate = jnp.dot(h, wg_ref[...].astype(bf16), preferred_element_type=jnp.float32)
        up = jnp.dot(h, wu_ref[...].astype(bf16), preferred_element_type=jnp.float32)
        act = (gate * jax.nn.sigmoid(gate) * up * 0.5).astype(bf16)
        return jnp.dot(act, wd_ref[...].astype(bf16), preferred_element_type=jnp.float32)

    def first_step():
        x = x_buf[...]
        h_ref[...] = _rms_normalize(x, g_ref[...]).astype(bf16)
        o_ref[...] = x + half_swiglu()

    _on_token_tile(x_hbm, x_buf, x_sem, first_step)

    @pl.when((f > 0) & (f < last) if final_norm else (f > 0))
    def _():
        o_ref[...] += half_swiglu()

    if final_norm:
        @pl.when(f == last)
        def _():
            o_ref[...] = _rms_normalize(o_ref[...] + half_swiglu(), gf_ref[...])


def _ffn(x, gain, w_gu, w_down, final_gain, *, final_norm):
    t, d = x.shape
    d_ff = w_down.shape[0]
    tm, tf = FFN_TOKEN_TILE, FFN_FF_TILE
    n_ff_tiles = d_ff // tf
    assert t % tm == 0 and d_ff % tf == 0 and n_ff_tiles >= 2
    kern = functools.partial(_ffn_kernel, n_ff_tiles=n_ff_tiles, final_norm=final_norm)
    return pl.pallas_call(
        kern,
        grid=(t // tm, n_ff_tiles),
        in_specs=[
            _X_IN_HBM,
            pl.BlockSpec((1, d), lambda i, f: (0, 0)),
            pl.BlockSpec((d, tf), lambda i, f: (0, f)),
            pl.BlockSpec((d, tf), lambda i, f: (0, f + n_ff_tiles)),
            pl.BlockSpec((tf, d), lambda i, f: (f, 0)),
            pl.BlockSpec((1, d), lambda i, f: (0, 0)),
        ],
        out_specs=pl.BlockSpec((tm, d), lambda i, f: (i, 0)),
        out_shape=jax.ShapeDtypeStruct((t, d), jnp.float32),
        scratch_shapes=[pltpu.VMEM((tm, d), jnp.bfloat16)] + _token_tile_scratch(tm, d),
        compiler_params=_compiler_params(_SEQUENTIAL_GRID),
        name="ffn_final" if final_norm else "ffn",
    )(x, gain, w_gu, w_gu, w_down, final_gain)


def _qkv_proj_kernel(x_hbm, g_ref, w_ref, s_ref, o_ref, h_ref, x_buf, x_sem):
    def start_tile():
        h_ref[...] = _rms_normalize(x_buf[...], g_ref[...]).astype(jnp.bfloat16)

    _on_token_tile(x_hbm, x_buf, x_sem, start_tile)

    acc = jnp.dot(h_ref[...], w_ref[...].astype(jnp.bfloat16), preferred_element_type=jnp.float32)
    o_ref[...] = (acc * s_ref[...]).astype(o_ref.dtype)


def _qkv_proj(x, gain, w_in, col_scale):
    t, d = x.shape
    tm, tn = PROJ_TOKEN_TILE, PROJ_COL_TILE
    assert t % tm == 0 and QKV_COLS % tn == 0 and QKV_COLS // tn >= 2
    return pl.pallas_call(
        _qkv_proj_kernel,
        grid=(t // tm, QKV_COLS // tn),
        in_specs=[
            _X_IN_HBM,
            pl.BlockSpec((1, d), lambda i, j: (0, 0)),
            pl.BlockSpec((d, tn), lambda i, j: (0, j)),
            pl.BlockSpec((1, tn), lambda i, j: (0, j)),
        ],
        out_specs=pl.BlockSpec((tm, tn), lambda i, j: (i, j)),
        out_shape=jax.ShapeDtypeStruct((t, QKV_COLS), jnp.bfloat16),
        scratch_shapes=[pltpu.VMEM((tm, d), jnp.bfloat16)] + _token_tile_scratch(tm, d),
        compiler_params=_compiler_params(_SEQUENTIAL_GRID),
        name="qkv_proj",
    )(x, gain, w_in, col_scale)


ALIBI_TERMS = 3


def _alibi_columns(slope, t):
    lane = lax.broadcasted_iota(jnp.int32, (t, LANES), 1)
    rest = lax.broadcasted_iota(jnp.int32, (t, LANES), 0).astype(jnp.float32) * slope
    k_extra = jnp.zeros((t, LANES), jnp.float32)
    for i in range(ALIBI_TERMS):
        term = rest.astype(jnp.bfloat16).astype(jnp.float32)
        k_extra = jnp.where(lane == i, term, k_extra)
        rest = rest - term
    q_extra = jnp.where(lane < ALIBI_TERMS, 1.0, 0.0)
    return k_extra.astype(jnp.bfloat16), q_extra.astype(jnp.bfloat16)


def _query_shift(slope, t):
    return lax.broadcasted_iota(jnp.int32, (1, t), 1).astype(jnp.float32) * (-slope)


def _causal_mask_t(t):
    kk = lax.broadcasted_iota(jnp.int32, (t, t), 0)
    qq = lax.broadcasted_iota(jnp.int32, (t, t), 1)
    return jnp.where(qq >= kk, 0.0, MASK_VALUE)


def _fold_rows(x, op):
    rows, cols = x.shape
    return op(x.reshape(rows // SUBLANES, SUBLANES, cols), axis=0)


def _score_tiles(k_tile, q, shifts, mask_diag, s_ref, result):
    n, t = len(shifts), q.shape[0]
    cand = None
    for j in range(n):
        sj = lax.dot_general(k_tile(j), q, _NT_DIMS, preferred_element_type=jnp.float32)
        if j == n - 1:
            sj = sj + mask_diag
        s_ref[j * t:(j + 1) * t, :] = sj
        cj = _fold_rows(sj, jnp.max) + shifts[j]
        cand = cj if cand is None else jnp.maximum(cand, cj)
        yield
    result.append(jnp.max(cand, axis=0, keepdims=True))


ONES_ROWS = 2 * SUBLANES


def _softmax_pv(s_ref, m, shifts, vt_tile, result, dv=None):
    t = m.shape[1]
    lpart, acc = None, None
    for j in range(len(shifts)):
        sj = s_ref[j * t:(j + 1) * t, :]
        p = jnp.exp2(sj - (m - shifts[j]))
        if dv is None:
            pj = _fold_rows(p, jnp.sum)
            lpart = pj if lpart is None else lpart + pj
        vt = vt_tile(j)
        pv = jnp.dot(vt, p.astype(vt.dtype), preferred_element_type=jnp.float32)
        acc = pv if acc is None else acc + pv
        yield
    if dv is None:
        result.append(acc / jnp.sum(lpart, axis=0, keepdims=True))
    else:
        result.append(acc[:dv] / acc[dv:dv + 1])


def _alternate(*steps):
    for _ in itertools.zip_longest(*steps):
        yield


def _run_pipelined(stage_a, stage_b, n, slots):
    ahead = slots - 1
    states = {}

    def run_a(u):
        if u < n:
            steps, states[u] = stage_a(u)
            for _ in steps:
                pass

    for u in range(ahead):
        run_a(u)
    for u in range(n):
        run_a(u + ahead)
        for _ in stage_b(u, states.pop(u)):
            pass


ATTN_HEADS_PER_STEP = 2


def _diff_attn_kernel(slopes_ref, lq1_ref, lk1_ref, lq2_ref, lk2_ref, gs_ref,
                      q_ref, k_ref, v_ref, o_ref, vt_ref, s_ref):
    t, hw, nh, slots = ATTN_TILE, 2 * DIFF_HEAD_DIM, ATTN_HEADS_PER_STEP, s_ref.shape[0]
    n_tiles = q_ref.shape[1] // t
    mask_diag = _causal_mask_t(t)
    lam = (jnp.exp(jnp.sum(lq1_ref[...] * lk1_ref[...], axis=-1, keepdims=True))
           - jnp.exp(jnp.sum(lq2_ref[...] * lk2_ref[...], axis=-1, keepdims=True))
           + LAM_INIT)
    lane = lax.broadcasted_iota(jnp.int32, (t, LANES), 1)

    def head_setup(hh):
        cols = slice(hh * hw, (hh + 1) * hw)
        slope = slopes_ref[pl.program_id(1) * nh + hh]
        vt_ref[hh, :hw] = v_ref[0, :, cols].T
        vt_ref[hh, hw:] = jnp.ones((ONES_ROWS, vt_ref.shape[2]), vt_ref.dtype)
        k_extra, q_extra = _alibi_columns(slope, t)
        q_shift = _query_shift(slope, t)
        return dict(
            cols=cols, q_extra=q_extra,
            shifts=lambda c: [q_shift - slope * float((c - j) * t) for j in range(c + 1)],
            k_tile=lambda j: jnp.concatenate([k_ref[0, j * t:(j + 1) * t, cols], k_extra], axis=1),
            vt_tile=lambda j: vt_ref[hh, :, j * t:(j + 1) * t])

    heads = [head_setup(hh) for hh in range(nh)]
    streams = [(hh, mi) for hh in range(nh) for mi in range(2)]

    def scores(c):
        maxes, steps = [], []
        for hh, mi in streams:
            hd = heads[hh]
            q = q_ref[0, c * t:(c + 1) * t, hd["cols"]]
            keep = (lane < DIFF_HEAD_DIM) if mi == 0 else (lane >= DIFF_HEAD_DIM)
            qm = jnp.where(keep, q, jnp.zeros_like(q))
            maxes.append([])
            steps.append(_score_tiles(hd["k_tile"], jnp.concatenate([qm, hd["q_extra"]], axis=1),
                                      hd["shifts"](c), mask_diag,
                                      s_ref.at[c % slots, 2 * hh + mi], maxes[-1]))
        return _alternate(*steps), maxes

    def finish(c, maxes):
        outs = [[] for _ in streams]
        yield from _alternate(*[
            _softmax_pv(s_ref.at[c % slots, 2 * hh + mi], maxes[si][0], heads[hh]["shifts"](c),
                        heads[hh]["vt_tile"], outs[si], dv=hw)
            for si, (hh, mi) in enumerate(streams)])
        for hh, hd in enumerate(heads):
            o = (outs[2 * hh][0] - lam * outs[2 * hh + 1][0]).T
            o_ref[0, c * t:(c + 1) * t, hd["cols"]] = (
                _rms_normalize(o, gs_ref[...]) * (1.0 - LAM_INIT)).astype(o_ref.dtype)

    _run_pipelined(scores, finish, n_tiles, slots)


MOBA_HEADS_PER_STEP = 2


def _moba_attn_kernel(slopes_ref, q_ref, k_ref, v_ref, *refs, n_blocks, n_sel, n_casts):
    cast_src, o_ref, cast_dst = refs[:n_casts], refs[n_casts], refs[n_casts + 1:2 * n_casts + 1]
    vt_ref, s_ref = refs[2 * n_casts + 1:]
    for src, dst in zip(cast_src, cast_dst):
        dst[...] = src[...].astype(dst.dtype)

    blk, dh, nh, slots = MOBA_BLOCK, MOBA_HEAD_DIM, MOBA_HEADS_PER_STEP, s_ref.shape[0]
    mask_diag = _causal_mask_t(blk)
    blk_id = lax.broadcasted_iota(jnp.int32, (n_blocks, blk), 0)

    def head_setup(hh):
        cols = slice(hh * dh, (hh + 1) * dh)
        slope = slopes_ref[pl.program_id(1) * nh + hh]
        vt_ref[hh] = v_ref[0, :, cols].T
        k_extra, q_extra = _alibi_columns(slope, blk)
        kmean = jnp.mean(k_ref[0, :, cols].astype(jnp.float32).reshape(n_blocks, blk, dh), axis=1)
        kmean = jnp.concatenate([kmean, jnp.zeros((LANES - n_blocks, dh), jnp.float32)], axis=0)
        kmean_hi = kmean.astype(jnp.bfloat16)
        return dict(
            cols=cols, slope=slope, q_extra=q_extra, q_shift=_query_shift(slope, blk),
            kmean_hi=kmean_hi, kmean_lo=(kmean - kmean_hi.astype(jnp.float32)).astype(jnp.bfloat16),
            k_tile=lambda j: jnp.concatenate([k_ref[0, j * blk:(j + 1) * blk, cols], k_extra], axis=1),
            vt_tile=lambda j: vt_ref[hh, :, j * blk:(j + 1) * blk])

    heads = [head_setup(hh) for hh in range(nh)]

    def head_scores(c, hh, hd):
        q = q_ref[0, c * blk:(c + 1) * blk, hd["cols"]]
        g = (lax.dot_general(hd["kmean_hi"], q, _NT_DIMS, preferred_element_type=jnp.float32)
             + lax.dot_general(hd["kmean_lo"], q, _NT_DIMS, preferred_element_type=jnp.float32))[:n_blocks]
        past = blk_id < c
        g = jnp.where(past, g, -jnp.inf)
        rank = jnp.zeros(g.shape, jnp.int32)
        for r in range(1, n_blocks):
            other = pltpu.roll(g, r, 0)
            rank += jnp.where(blk_id >= r, (other >= g).astype(jnp.int32), (other > g).astype(jnp.int32))
        keep = (past & (rank < n_sel)) | (blk_id == c)
        sel_bias = jnp.where(keep, 0.0, MASK_VALUE)
        shifts = [sel_bias[j:j + 1, :] + (hd["q_shift"] - hd["slope"] * float((c - j) * blk))
                  for j in range(c + 1)]
        q_aug = jnp.concatenate([q, hd["q_extra"]], axis=1)
        m = []
        steps = _score_tiles(hd["k_tile"], q_aug, shifts, mask_diag, s_ref.at[c % slots, hh], m)
        return steps, (m, shifts)

    def scores(c):
        per_head = [head_scores(c, hh, hd) for hh, hd in enumerate(heads)]
        return _alternate(*[steps for steps, _ in per_head]), [state for _, state in per_head]

    def finish(c, states):
        outs = [[] for _ in heads]
        yield from _alternate(*[
            _softmax_pv(s_ref.at[c % slots, hh], m[0], shifts, heads[hh]["vt_tile"], outs[hh])
            for hh, (m, shifts) in enumerate(states)])
        for hh, hd in enumerate(heads):
            o_ref[0, c * blk:(c + 1) * blk, hd["cols"]] = outs[hh][0].T.astype(o_ref.dtype)

    _run_pipelined(scores, finish, n_blocks, slots)


MIXERS_SCORE_SLOTS = 2


def _mixers_kernel(slopes_a_ref, slopes_b_ref, lq1_ref, lk1_ref, lq2_ref, lk2_ref, gs_ref,
                   qa_ref, ka_ref, va_ref, qb_ref, kb_ref, vb_ref, *refs, n_blocks, n_sel, n_casts):
    cast_src = refs[:n_casts]
    oa_ref, ob_ref = refs[n_casts], refs[n_casts + 1]
    cast_dst = refs[n_casts + 2:2 * n_casts + 2]
    vta_ref, sa_ref, vtb_ref, sb_ref = refs[2 * n_casts + 2:]
    _diff_attn_kernel(slopes_a_ref, lq1_ref, lk1_ref, lq2_ref, lk2_ref, gs_ref,
                      qa_ref, ka_ref, va_ref, oa_ref, vta_ref, sa_ref)
    _moba_attn_kernel(slopes_b_ref, qb_ref, kb_ref, vb_ref, *cast_src, ob_ref, *cast_dst, vtb_ref, sb_ref,
                      n_blocks=n_blocks, n_sel=n_sel, n_casts=n_casts)


def _mixers(qkv, slopes_a, slopes_b, lam_q1, lam_k1, lam_q2, lam_k2, g_subln, w_in, row_weights):
    b, s, _ = qkv.shape
    hps = ATTN_HEADS_PER_STEP
    assert hps == MOBA_HEADS_PER_STEP and N_HEADS_DIFF == N_HEADS_MOBA
    nh, hw, dh, blk = N_HEADS_DIFF, 2 * DIFF_HEAD_DIM, MOBA_HEAD_DIM, MOBA_BLOCK
    assert hw == LANES and dh == LANES and s % ATTN_TILE == 0 and s % blk == 0 and nh % hps == 0
    n_blocks = s // blk
    assert n_blocks == SUBLANES, "block ranking uses one vreg row per MoBA block"
    n_sel = min(MOBA_TOPK, n_blocks - 1)
    w = hps * LANES
    n_g = nh // hps
    ka0, va0 = DIFF_WIDTH // w, 2 * DIFF_WIDTH // w
    qb0 = 3 * DIFF_WIDTH // w
    kb0, vb0 = qb0 + n_g, qb0 + 2 * n_g
    head_block = lambda first: pl.BlockSpec((1, s, w), lambda bi, g: (bi, 0, first + g))
    lam_spec = pl.BlockSpec((1, DIFF_HEAD_DIM), lambda bi, g: (0, 0))

    n_steps = b * n_g
    step = lambda bi, g: bi * n_g + g
    d, gate_cols = w_in.shape[0], w_in.shape[1] - QKV_COLS
    gc = gate_cols // n_steps
    assert gate_cols % n_steps == 0 and gc % LANES == 0 and QKV_COLS % gc == 0
    cast_in = [pl.BlockSpec((d, gc), lambda bi, g: (0, QKV_COLS // gc + step(bi, g)))]
    cast_out = [pl.BlockSpec((d, gc), lambda bi, g: (0, step(bi, g)))]
    cast_shapes = [jax.ShapeDtypeStruct((d, gate_cols), jnp.bfloat16)]
    for m in row_weights:
        rows = m.shape[0] // n_steps
        assert m.shape[0] % n_steps == 0 and rows % (2 * SUBLANES) == 0
        spec = pl.BlockSpec((rows, m.shape[1]), lambda bi, g: (step(bi, g), 0))
        cast_in.append(spec)
        cast_out.append(spec)
        cast_shapes.append(jax.ShapeDtypeStruct(m.shape, jnp.bfloat16))

    kern = functools.partial(_mixers_kernel, n_blocks=n_blocks, n_sel=n_sel, n_casts=len(cast_in))
    o_a, o_b, w_gates, *row_weights_bf = pl.pallas_call(
        kern,
        grid=(b, n_g),
        in_specs=[
            pl.BlockSpec(memory_space=pltpu.SMEM), pl.BlockSpec(memory_space=pltpu.SMEM),
            lam_spec, lam_spec, lam_spec, lam_spec,
            pl.BlockSpec((1, hw), lambda bi, g: (0, 0)),
            head_block(0), head_block(ka0), head_block(va0),
            head_block(qb0), head_block(kb0), head_block(vb0),
        ] + cast_in,
        out_specs=[head_block(0), head_block(0)] + cast_out,
        out_shape=[jax.ShapeDtypeStruct((b, s, DIFF_WIDTH), jnp.bfloat16),
                   jax.ShapeDtypeStruct((b, s, MOBA_WIDTH), jnp.bfloat16)] + cast_shapes,
        scratch_shapes=[pltpu.VMEM((hps, hw + ONES_ROWS, s), jnp.bfloat16),
                        pltpu.VMEM((MIXERS_SCORE_SLOTS, 2 * hps, s, ATTN_TILE), jnp.float32),
                        pltpu.VMEM((hps, dh, s), jnp.bfloat16),
                        pltpu.VMEM((MIXERS_SCORE_SLOTS, hps, s, blk), jnp.float32)],
        compiler_params=_compiler_params(("parallel", "parallel")),
        name="mixers",
    )(slopes_a, slopes_b, lam_q1, lam_k1, lam_q2, lam_k2, g_subln, qkv, qkv, qkv, qkv, qkv, qkv,
      w_in, *row_weights)
    return o_a, o_b, w_gates, row_weights_bf


def _mix_out_kernel(x_hbm, g_ref, oa_ref, ob_ref, wga_ref, wgb_ref, pa_ref, pb_ref, wo_ref, o_ref,
                    h_ref, x_buf, x_sem):
    def mixed():
        h = h_ref[...]
        gate_a = jnp.dot(h, wga_ref[...], preferred_element_type=jnp.float32)
        gate_b = jnp.dot(h, wgb_ref[...], preferred_element_type=jnp.float32)
        proj_a = jnp.dot(oa_ref[...], pa_ref[...], preferred_element_type=jnp.float32)
        proj_b = jnp.dot(ob_ref[...], pb_ref[...], preferred_element_type=jnp.float32)
        merged = jax.nn.sigmoid(gate_a) * proj_a + jax.nn.sigmoid(gate_b) * proj_b
        return jnp.dot(merged.astype(jnp.bfloat16), wo_ref[...], preferred_element_type=jnp.float32)

    def first_step():
        x = x_buf[...]
        h_ref[...] = _rms_normalize(x, g_ref[...]).astype(jnp.bfloat16)
        o_ref[...] = x + mixed()

    _on_token_tile(x_hbm, x_buf, x_sem, first_step)

    @pl.when(pl.program_id(1) > 0)
    def _():
        o_ref[...] += mixed()


def _mix_out(x, gain, o_a, o_b, w_gates, p_a, p_b, w_o):
    t, d = x.shape
    tm, tc = MIX_TOKEN_TILE, MIX_COL_TILE
    assert t % tm == 0 and d % tc == 0 and d // tc >= 2
    ga_blk0, gb_blk0 = 0, d // tc
    return pl.pallas_call(
        _mix_out_kernel,
        grid=(t // tm, d // tc),
        in_specs=[
            _X_IN_HBM,
            pl.BlockSpec((1, d), lambda i, c: (0, 0)),
            pl.BlockSpec((tm, DIFF_WIDTH), lambda i, c: (i, 0)),
            pl.BlockSpec((tm, MOBA_WIDTH), lambda i, c: (i, 0)),
            pl.BlockSpec((d, tc), lambda i, c: (0, ga_blk0 + c)),
            pl.BlockSpec((d, tc), lambda i, c: (0, gb_blk0 + c)),
            pl.BlockSpec((DIFF_WIDTH, tc), lambda i, c: (0, c)),
            pl.BlockSpec((MOBA_WIDTH, tc), lambda i, c: (0, c)),
            pl.BlockSpec((tc, d), lambda i, c: (c, 0)),
        ],
        out_specs=pl.BlockSpec((tm, d), lambda i, c: (i, 0)),
        out_shape=jax.ShapeDtypeStruct((t, d), jnp.float32),
        scratch_shapes=[pltpu.VMEM((tm, d), jnp.bfloat16)] + _token_tile_scratch(tm, d),
        compiler_params=_compiler_params(_SEQUENTIAL_GRID),
        name="mix_out",
    )(x, gain, o_a, o_b, w_gates, w_gates, p_a, p_b, w_o)


LOG2_E = math.log2(math.e)


def _alibi_slopes(n):
    return jnp.asarray(LOG2_E * 2.0 ** (-8.0 * np.arange(1, n + 1) / n), dtype=jnp.float32)


def _qkv_col_scale():
    scale = np.ones((1, QKV_COLS), np.float32)
    scale[:, :DIFF_WIDTH] = LOG2_E * DIFF_HEAD_DIM ** -0.5
    scale[:, 3 * DIFF_WIDTH:3 * DIFF_WIDTH + MOBA_WIDTH] = LOG2_E * MOBA_HEAD_DIM ** -0.5
    return jnp.asarray(scale)


def kernel(x, g_ffn1, w_ffn1_gu, w_ffn1_down, g_mix, w_in, lam_q1, lam_k1, lam_q2, lam_k2, g_subln, p_a, p_b, w_o, g_ffn2, w_ffn2_gu, w_ffn2_down, g_final):
    b, s, d = x.shape
    assert g_ffn1.shape[0] == 1, "single-layer stack"
    xt = x.reshape(b * s, d)
    g_final_row = g_final.reshape(1, d)

    x1 = _ffn(xt, g_ffn1, w_ffn1_gu[0], w_ffn1_down[0], g_final_row, final_norm=False)

    qkv = _qkv_proj(x1, g_mix, w_in[0], _qkv_col_scale()).reshape(b, s, QKV_COLS)
    o_a, o_b, w_gates, (p_a_bf, p_b_bf, w_o_bf) = _mixers(
        qkv, _alibi_slopes(N_HEADS_DIFF), _alibi_slopes(N_HEADS_MOBA), lam_q1, lam_k1, lam_q2, lam_k2,
        g_subln, w_in[0], (p_a[0], p_b[0], w_o[0]))
    x2 = _mix_out(x1, g_mix, o_a.reshape(b * s, DIFF_WIDTH), o_b.reshape(b * s, MOBA_WIDTH),
                  w_gates, p_a_bf, p_b_bf, w_o_bf)

    out = _ffn(x2, g_ffn2, w_ffn2_gu[0], w_ffn2_down[0], g_final_row, final_norm=True)
    return out.reshape(b, s, d)
```
